```python
import numpy as np
import jax, jax.numpy as jnp
from jax import lax

D_MODEL = 1024
BATCH = 8
SEQ = 4096
DEPTH = 2

MEM_LEN = 256
NSA_HEADS = 8
NSA_KV_HEADS = 2
NSA_GROUP = NSA_HEADS // NSA_KV_HEADS
NSA_HEAD_DIM = D_MODEL // 16
NSA_WIDTH = NSA_HEADS * NSA_HEAD_DIM
N_BRANCH = 3
CMP_LEN = 32
CMP_STRIDE = 16
CMP_HIDDEN = 2 * NSA_HEAD_DIM
SEL_BLOCK = 64
SEL_TOP = 16
WINDOW = 512
NSA_Q_BLOCK = 64
FORCE_BONUS = 100.0
KV_WIDTH = N_BRANCH * 2 * NSA_KV_HEADS * NSA_HEAD_DIM
GATE_WIDTH = NSA_HEADS * N_BRANCH
POOL_WINDOWS = (2, 4, 8, 16)
POOL_GROUPS = len(POOL_WINDOWS)
POOL_WIDTH = D_MODEL // 2
POOL_GROUP_DIM = POOL_WIDTH // POOL_GROUPS
AB_IN = NSA_WIDTH + KV_WIDTH + GATE_WIDTH + POOL_WIDTH
AB_MIX = NSA_WIDTH + POOL_WIDTH
SG_CHUNK = 128
SG_GROUPS = 8
SG_WIDTH = D_MODEL
SG_GROUP_DIM = SG_WIDTH // SG_GROUPS
MEM_HEADS = 4
MEM_HEAD_DIM = D_MODEL // MEM_HEADS
D_FF = 256 * ((8 * D_MODEL // 3 + 255) // 256)
CONV_WIDTH = 3
ALPHA = (2 * DEPTH) ** 0.25
BETA = (8 * DEPTH) ** -0.25
LN_EPS = 1e-5
NEG_INF = -1e30
N_EVEN = (DEPTH + 1) // 2
N_ODD = DEPTH // 2

kernel_name = "hybrid_nsa_pool_sgu_deepnorm"


def layer_norm(x, g, b):
    xf = x.astype(jnp.float32)
    mu = jnp.mean(xf, -1, keepdims=True)
    var = jnp.mean(jnp.square(xf - mu), -1, keepdims=True)
    return ((xf - mu) * lax.rsqrt(var + LN_EPS) * g + b).astype(x.dtype)


def _gather_blocks(blocks, idx):
    return jax.vmap(jax.vmap(lambda blk, ix: blk[ix]))(blocks, idx)


def nsa_mixer(q, kv, gates, cmp_pe, cmp_w1, cmp_w2):
    B, S = q.shape[0], q.shape[1]
    G, R, dh = NSA_KV_HEADS, NSA_GROUP, NSA_HEAD_DIM
    dt = q.dtype
    n_cmp = (S - CMP_LEN) // CMP_STRIDE + 1
    n_sel = S // SEL_BLOCK
    n_top = min(SEL_TOP, n_sel)
    n_qb = S // NSA_Q_BLOCK
    qh = (q * dh ** -0.5).reshape(B, S, G, R, dh).transpose(0, 2, 3, 1, 4)
    gate_h = jax.nn.sigmoid(gates).reshape(B, S, G, R, N_BRANCH).transpose(0, 2, 3, 1, 4)

    blk_idx = np.arange(n_cmp)[:, None] * CMP_STRIDE + np.arange(CMP_LEN)[None, :]
    blocks = kv[:, :, 0][:, blk_idx]
    blocks = blocks + cmp_pe.transpose(1, 0, 2)[None, None, :, :, None, :]
    blocks = blocks.transpose(0, 3, 4, 1, 2, 5).reshape(B, 2, G, n_cmp, CMP_LEN * dh)
    hid = jax.nn.gelu(jnp.einsum('bkgnf,kfh->bkgnh', blocks, cmp_w1))
    kv_cmp = jnp.einsum('bkgnh,khd->bkgnd', hid, cmp_w2)
    k_cmp, v_cmp = kv_cmp[:, 0], kv_cmp[:, 1]
    cmp_start = np.arange(n_cmp) * CMP_STRIDE
    cmp_end = cmp_start + CMP_LEN - 1
    sel_start = np.arange(n_sel) * SEL_BLOCK
    overlap = jnp.asarray(((cmp_start[:, None] < sel_start[None, :] + SEL_BLOCK)
                           & (cmp_start[:, None] + CMP_LEN > sel_start[None, :])).astype(np.float32))

    kv_s = kv[:, :, 1].transpose(0, 2, 3, 1, 4).reshape(B, 2, G, n_sel, SEL_BLOCK, dh)
    k_sel, v_sel = kv_s[:, 0], kv_s[:, 1]
    kv_w = jnp.pad(kv[:, :, 2].transpose(0, 2, 3, 1, 4), ((0, 0), (0, 0), (0, 0), (WINDOW, 0), (0, 0)))
    k_win, v_win = kv_w[:, 0], kv_w[:, 1]

    j_idx = jnp.arange(n_sel)

    def query_block(c):
        t0 = c * NSA_Q_BLOCK
        tpos = t0 + jnp.arange(NSA_Q_BLOCK)
        qc = lax.dynamic_slice_in_dim(qh, t0, NSA_Q_BLOCK, axis=3)
        gc = lax.dynamic_slice_in_dim(gate_h, t0, NSA_Q_BLOCK, axis=3)
        s = jnp.einsum('bgrqd,bgnd->bgrqn', qc, k_cmp).astype(jnp.float32)
        valid = cmp_end[None, :] <= tpos[:, None]
        p_cmp = jax.nn.softmax(jnp.where(valid, s, NEG_INF), axis=-1) * (tpos >= CMP_LEN - 1)[:, None]
        o_cmp = jnp.einsum('bgrqn,bgnd->bgrqd', p_cmp.astype(dt), v_cmp)
        imp = jnp.einsum('bgrqn,nj->bgqj', p_cmp, overlap)
        cur = tpos // SEL_BLOCK
        blk_ok = j_idx[None, :] * SEL_BLOCK <= tpos[:, None]
        forced = (j_idx[None, :] == 0) | (j_idx[None, :] == cur[:, None]) | (j_idx[None, :] == cur[:, None] - 1)
        score = jnp.where(blk_ok, imp + FORCE_BONUS * forced, NEG_INF)
        _, sel = lax.top_k(score, n_top)
        k_g = _gather_blocks(k_sel, sel)
        v_g = _gather_blocks(v_sel, sel)
        kpos = sel[..., None] * SEL_BLOCK + jnp.arange(SEL_BLOCK)
        s = jnp.einsum('bgrqd,bgqnld->bgrqnl', qc, k_g).astype(jnp.float32)
        s = jnp.where((kpos <= tpos[:, None, None])[:, :, None], s, NEG_INF)
        p = jax.nn.softmax(s.reshape(s.shape[:4] + (-1,)), axis=-1).reshape(s.shape)
        o_sel = jnp.einsum('bgrqnl,bgqnld->bgrqd', p.astype(dt), v_g)
        kw = lax.dynamic_slice_in_dim(k_win, t0, NSA_Q_BLOCK + WINDOW, axis=2)
        vw = lax.dynamic_slice_in_dim(v_win, t0, NSA_Q_BLOCK + WINDOW, axis=2)
        wpos = t0 - WINDOW + jnp.arange(NSA_Q_BLOCK + WINDOW)
        wmask = (wpos[None, :] <= tpos[:, None]) & (wpos[None, :] > tpos[:, None] - WINDOW) & (wpos[None, :] >= 0)
        s = jnp.einsum('bgrqd,bgkd->bgrqk', qc, kw).astype(jnp.float32)
        p = jax.nn.softmax(jnp.where(wmask, s, NEG_INF), axis=-1)
        o_win = jnp.einsum('bgrqk,bgkd->bgrqd', p.astype(dt), vw)
        return gc[..., 0:1] * o_cmp + gc[..., 1:2] * o_sel + gc[..., 2:3] * o_win

    out = lax.map(query_block, jnp.arange(n_qb))
    return out.transpose(1, 0, 4, 2, 3, 5).reshape(B, S, NSA_HEADS * dh)


def pool_mixer(u, pool_w, pool_scale):
    B, S = u.shape[0], u.shape[1]
    uf = u.astype(jnp.float32).reshape(B, S, POOL_GROUPS, POOL_GROUP_DIM)
    t = jnp.arange(S)
    outs = []
    for gi, w in enumerate(POOL_WINDOWS):
        ug = uf[:, :, gi]
        cs = jnp.cumsum(ug, axis=1)
        lag = jnp.pad(cs, ((0, 0), (w, 0), (0, 0)))[:, :S]
        cnt = jnp.minimum(t + 1, w).astype(jnp.float32)[:, None]
        outs.append((cs - lag) / cnt - ug)
    pooled = jnp.stack(outs, axis=2).astype(u.dtype)
    y = jnp.einsum('bsgc,gcd->bsgd', pooled, pool_w).reshape(B, S, POOL_WIDTH)
    return y * pool_scale


def spatial_gating(x, w_in, norm_g, norm_b, w_s, b_s):
    B, S = x.shape[0], x.shape[1]
    z = jax.nn.gelu(x @ w_in)
    u, v = z[..., :SG_WIDTH], z[..., SG_WIDTH:]
    v = layer_norm(v, norm_g, norm_b).reshape(B, S // SG_CHUNK, SG_CHUNK, SG_GROUPS, SG_GROUP_DIM)
    causal = jnp.tril(jnp.ones((SG_CHUNK, SG_CHUNK), w_s.dtype))
    sv = jnp.einsum('hts,bcshd->bcthd', w_s * causal, v) + b_s.T[None, None, :, :, None]
    return u * sv.reshape(B, S, SG_WIDTH)


def memory_attention(x, mem, wq, wkv, wo):
    B, S = x.shape[0], x.shape[1]
    M = mem.shape[1]
    q = (x @ wq).reshape(B, S, MEM_HEADS, MEM_HEAD_DIM) * MEM_HEAD_DIM ** -0.5
    kv = (mem @ wkv).reshape(B, M, 2, MEM_HEADS, MEM_HEAD_DIM)
    s = jnp.einsum('bshd,bmhd->bhsm', q, kv[:, :, 0]).astype(jnp.float32)
    p = jax.nn.softmax(s, axis=-1).astype(x.dtype)
    o = jnp.einsum('bhsm,bmhd->bshd', p, kv[:, :, 1]).reshape(B, S, MEM_HEADS * MEM_HEAD_DIM)
    return o @ wo


def conv_ffn(x, w_up, conv_w, conv_b, w_down):
    h = x @ w_up
    h = lax.conv_general_dilated(h, conv_w[:, None, :], window_strides=(1,), padding=[(CONV_WIDTH - 1, 0)],
                                 dimension_numbers=('NWC', 'WIO', 'NWC'), feature_group_count=2 * D_FF) + conv_b
    a, g = h[..., :D_FF], h[..., D_FF:]
    return (a * jax.nn.silu(g)) @ w_down


def setup_inputs(seed: int = 0) -> dict:
    key = jax.random.key(seed)
    ks = iter(jax.random.split(key, 32))

    def nrm(shape, scale):
        return jax.random.normal(next(ks), shape, jnp.float32) * scale

    D = D_MODEL
    cmp_flat = CMP_LEN * NSA_HEAD_DIM
    return {
        'x': nrm((BATCH, SEQ, D), 1.0),
        'mem': nrm((BATCH, MEM_LEN, D), 1.0),
        'ab_w_in': nrm((N_EVEN, D, AB_IN), D ** -0.5),
        'nsa_cmp_pe': nrm((N_EVEN, 2, CMP_LEN, NSA_HEAD_DIM), 0.1),
        'nsa_cmp_w1': nrm((N_EVEN, 2, cmp_flat, CMP_HIDDEN), cmp_flat ** -0.5),
        'nsa_cmp_w2': nrm((N_EVEN, 2, CMP_HIDDEN, NSA_HEAD_DIM), CMP_HIDDEN ** -0.5),
        'pool_w': nrm((N_EVEN, POOL_GROUPS, POOL_GROUP_DIM, POOL_GROUP_DIM), POOL_GROUP_DIM ** -0.5),
        'pool_scale': 1.0 + nrm((N_EVEN, POOL_WIDTH), 0.1),
        'ab_w_out': nrm((N_EVEN, AB_MIX, D), BETA * AB_MIX ** -0.5),
        'sg_w_in': nrm((N_ODD, D, 2 * SG_WIDTH), D ** -0.5),
        'sg_norm_g': 1.0 + nrm((N_ODD, SG_WIDTH), 0.1),
        'sg_norm_b': nrm((N_ODD, SG_WIDTH), 0.1),
        'sg_w_s': nrm((N_ODD, SG_GROUPS, SG_CHUNK, SG_CHUNK), SG_CHUNK ** -0.5),
        'sg_b_s': 1.0 + nrm((N_ODD, SG_GROUPS, SG_CHUNK), 0.1),
        'sg_w_out': nrm((N_ODD, SG_WIDTH, D), BETA * SG_WIDTH ** -0.5),
        'ln_g': 1.0 + nrm((DEPTH, 3, D), 0.1),
        'ln_b': nrm((DEPTH, 3, D), 0.1),
        'mem_wq': nrm((DEPTH, D, MEM_HEADS * MEM_HEAD_DIM), D ** -0.5),
        'mem_wkv': nrm((DEPTH, D, 2 * MEM_HEADS * MEM_HEAD_DIM), D ** -0.5),
        'mem_wo': nrm((DEPTH, MEM_HEADS * MEM_HEAD_DIM, D), BETA * D ** -0.5),
        'ffn_w_up': nrm((DEPTH, D, 2 * D_FF), D ** -0.5),
        'ffn_conv_w': nrm((DEPTH, CONV_WIDTH, 2 * D_FF), CONV_WIDTH ** -0.5),
        'ffn_conv_b': nrm((DEPTH, 2 * D_FF), 0.1),
        'ffn_w_down': nrm((DEPTH, D_FF, D), BETA * D_FF ** -0.5),
    }


def reference(x, mem, ab_w_in, nsa_cmp_pe, nsa_cmp_w1, nsa_cmp_w2, pool_w, pool_scale, ab_w_out,
              sg_w_in, sg_norm_g, sg_norm_b, sg_w_s, sg_b_s, sg_w_out, ln_g, ln_b,
              mem_wq, mem_wkv, mem_wo, ffn_w_up, ffn_conv_w, ffn_conv_b, ffn_w_down):
    B, S = x.shape[0], x.shape[1]
    o_q = NSA_WIDTH
    o_kv = o_q + KV_WIDTH
    o_g = o_kv + GATE_WIDTH
    for layer in range(DEPTH):
        i = layer // 2
        if layer % 2 == 0:
            proj = x @ ab_w_in[i]
            q = proj[..., :o_q].reshape(B, S, NSA_HEADS, NSA_HEAD_DIM)
            kv = proj[..., o_q:o_kv].reshape(B, S, N_BRANCH, 2, NSA_KV_HEADS, NSA_HEAD_DIM)
            gates = proj[..., o_kv:o_g].reshape(B, S, NSA_HEADS, N_BRANCH)
            u = proj[..., o_g:]
            mixed = jnp.concatenate([
                nsa_mixer(q, kv, gates, nsa_cmp_pe[i], nsa_cmp_w1[i], nsa_cmp_w2[i]),
                pool_mixer(u, pool_w[i], pool_scale[i]),
            ], axis=-1)
            y = mixed @ ab_w_out[i]
        else:
            y = spatial_gating(x, sg_w_in[i], sg_norm_g[i], sg_norm_b[i], sg_w_s[i], sg_b_s[i]) @ sg_w_out[i]
        x = layer_norm(ALPHA * x + y, ln_g[layer, 0], ln_b[layer, 0])
        x = layer_norm(ALPHA * x + memory_attention(x, mem, mem_wq[layer], mem_wkv[layer], mem_wo[layer]),
                       ln_g[layer, 1], ln_b[layer, 1])
        x = layer_norm(ALPHA * x + conv_ffn(x, ffn_w_up[layer], ffn_conv_w[layer], ffn_conv_b[layer], ffn_w_down[layer]),
                       ln_g[layer, 2], ln_b[layer, 2])
    return x
```

```python
import functools

import numpy as np
import jax
import jax.numpy as jnp
from jax import lax
from jax.experimental import pallas as pl
from jax.experimental.pallas import tpu as pltpu

F32 = jnp.float32
BF16 = jnp.bfloat16

LANES = 128
SUBLANES = 8
VMEM_LIMIT_BYTES = 56 * 1024 * 1024

D_MODEL = 1024
DEPTH = 2
NSA_HEADS = 8
NSA_KV_HEADS = 2
NSA_GROUP = NSA_HEADS // NSA_KV_HEADS
NSA_HEAD_DIM = 64
NSA_WIDTH = NSA_HEADS * NSA_HEAD_DIM
N_BRANCH = 3
CMP_LEN = 32
CMP_STRIDE = 16
CMP_HIDDEN = 2 * NSA_HEAD_DIM
SEL_BLOCK = 64
SEL_TOP = 16
WINDOW = 512
FORCE_BONUS = 100.0
KV_WIDTH = N_BRANCH * 2 * NSA_KV_HEADS * NSA_HEAD_DIM
GATE_WIDTH = NSA_HEADS * N_BRANCH
POOL_WINDOWS = (2, 4, 8, 16)
POOL_WIDTH = D_MODEL // 2
POOL_GROUP_DIM = POOL_WIDTH // len(POOL_WINDOWS)
SG_CHUNK = 128
SG_GROUPS = 8
SG_WIDTH = D_MODEL
SG_GROUP_DIM = SG_WIDTH // SG_GROUPS
MEM_HEADS = 4
MEM_HEAD_DIM = D_MODEL // MEM_HEADS
D_FF = 256 * ((8 * D_MODEL // 3 + 255) // 256)
CONV_WIDTH = 3
ALPHA = (2 * DEPTH) ** 0.25
LN_EPS = 1e-5
NEG_INF = -1e30

ROW_TILE = 512
NSA_Q_TILE = 128
SEL_KEY_CHUNK = 512
FFN_CHUNK = 256
POOL_HALO = 16
CONV_HALO = SUBLANES


def _params(n_parallel):
    return pltpu.CompilerParams(
        dimension_semantics=("parallel",) * n_parallel,
        vmem_limit_bytes=VMEM_LIMIT_BYTES,
    )


def _layer_norm(r, g, b):
    mu = jnp.mean(r, axis=-1, keepdims=True)
    d = r - mu
    var = jnp.mean(d * d, axis=-1, keepdims=True)
    return d * lax.rsqrt(var + LN_EPS) * g + b


def _gelu(x):
    c = np.sqrt(2.0 / np.pi).astype(np.float32)
    return 0.5 * x * (1.0 + jnp.tanh(c * (x + 0.044715 * (x * x * x))))


def _dot(a, b):
    return jnp.dot(a, b, preferred_element_type=F32)


def _dot_nt(a, b):
    return lax.dot_general(a, b, (((1,), (1,)), ((), ())), preferred_element_type=F32)


Q_COLS = NSA_HEADS * LANES
KV0_COLS = NSA_KV_HEADS * LANES
KV12_COLS = 2 * NSA_KV_HEADS * LANES
GATE_COLS = NSA_KV_HEADS * LANES
IN_PROJ_SPLITS = np.cumsum([0, Q_COLS, KV0_COLS, KV12_COLS, GATE_COLS, POOL_WIDTH])


def _in_proj_kernel(x_ref, w_ref, q_ref, kv0_ref, kv12_ref, gate_ref, u_ref):
    xb = x_ref[...].astype(BF16)
    outs = (q_ref, kv0_ref, kv12_ref, gate_ref, u_ref)
    for k, o_ref in enumerate(outs):
        lo, hi = int(IN_PROJ_SPLITS[k]), int(IN_PROJ_SPLITS[k + 1])
        o_ref[...] = _dot(xb, w_ref[:, lo:hi]).astype(o_ref.dtype)


def _in_proj(x2d, w):
    t = x2d.shape[0]
    n = w.shape[1]
    widths = [Q_COLS, KV0_COLS, KV12_COLS, GATE_COLS, POOL_WIDTH]
    dtypes = [BF16, F32, BF16, F32, F32]
    return pl.pallas_call(
        _in_proj_kernel,
        grid=(t // ROW_TILE,),
        in_specs=[
            pl.BlockSpec((ROW_TILE, D_MODEL), lambda i: (i, 0)),
            pl.BlockSpec((D_MODEL, n), lambda i: (0, 0)),
        ],
        out_specs=[pl.BlockSpec((ROW_TILE, wd), lambda i: (i, 0)) for wd in widths],
        out_shape=[jax.ShapeDtypeStruct((t, wd), dt) for wd, dt in zip(widths, dtypes)],
        compiler_params=_params(1),
        name="in_proj",
    )(x2d, w)


def _prep_in_proj_weight(w):
    d = w.shape[0]
    o_q = NSA_WIDTH
    o_kv = o_q + KV_WIDTH
    o_g = o_kv + GATE_WIDTH
    scale = NSA_HEAD_DIM ** -0.5
    wq = (w[:, :o_q] * scale).reshape(d, NSA_HEADS, NSA_HEAD_DIM)
    wq = jnp.pad(wq, ((0, 0), (0, 0), (0, LANES - NSA_HEAD_DIM))).reshape(d, Q_COLS)
    wkv = w[:, o_q:o_kv].reshape(d, N_BRANCH, 2, NSA_KV_HEADS, NSA_HEAD_DIM).transpose(0, 1, 3, 2, 4)
    wkv0 = wkv[:, 0].reshape(d, KV0_COLS)
    wkv12 = wkv[:, 1:].reshape(d, KV12_COLS)
    wg = w[:, o_kv:o_g].reshape(d, NSA_KV_HEADS, NSA_GROUP * N_BRANCH)
    wg = jnp.pad(wg, ((0, 0), (0, 0), (0, LANES - NSA_GROUP * N_BRANCH))).reshape(d, GATE_COLS)
    wu = w[:, o_g:]
    return jnp.concatenate([wq, wkv0, wkv12, wg, wu], axis=1).astype(BF16)


def _cmp_kv_kernel(x_ref, pe_ref, w1_ref, w2_ref, o_ref):
    n_chunks = x_ref.shape[3]
    out = None
    for kv in range(2):
        x = x_ref[0, 0, kv]
        lo = _dot((x + pe_ref[kv, 0:1, :]).astype(BF16), w1_ref[kv, 0])
        hi = _dot((x + pe_ref[kv, 1:2, :]).astype(BF16), w1_ref[kv, 1])
        hid = _gelu(lo + pltpu.roll(hi, n_chunks - 1, 0))
        part = _dot(hid.astype(BF16), w2_ref[kv])
        out = part if out is None else out + part
    o_ref[0, 0] = out.astype(o_ref.dtype)


def _cmp_kv(kv0, pe, w1, w2, batch, seq):
    n_chunks = seq // CMP_STRIDE
    half = CMP_STRIDE * NSA_HEAD_DIM
    x = kv0.reshape(batch, n_chunks, CMP_STRIDE, NSA_KV_HEADS, 2, NSA_HEAD_DIM)
    x = x.transpose(0, 3, 4, 1, 2, 5).reshape(batch, NSA_KV_HEADS, 2, n_chunks, half)
    pe2 = pe.reshape(2, 2, half)
    w1b = w1.reshape(2, 2, half, CMP_HIDDEN).astype(BF16)
    zeros = jnp.zeros_like(w2[0])
    w2p = jnp.stack([jnp.concatenate([w2[0], zeros], axis=1),
                     jnp.concatenate([zeros, w2[1]], axis=1)]).astype(BF16)
    return pl.pallas_call(
        _cmp_kv_kernel,
        grid=(batch, NSA_KV_HEADS),
        in_specs=[
            pl.BlockSpec((1, 1, 2, n_chunks, half), lambda b, g: (b, g, 0, 0, 0)),
            pl.BlockSpec((2, 2, half), lambda b, g: (0, 0, 0)),
            pl.BlockSpec((2, 2, half, CMP_HIDDEN), lambda b, g: (0, 0, 0, 0)),
            pl.BlockSpec((2, CMP_HIDDEN, LANES), lambda b, g: (0, 0, 0)),
        ],
        out_specs=pl.BlockSpec((1, 1, n_chunks, LANES), lambda b, g: (b, g, 0, 0)),
        out_shape=jax.ShapeDtypeStruct((batch, NSA_KV_HEADS, n_chunks, LANES), BF16),
        compiler_params=_params(2),
        name="cmp_kv",
    )(x, pe2, w1b, w2p)


def _split3(x):
    a = x.astype(BF16)
    r = x - a.astype(F32)
    b = r.astype(BF16)
    c = (r - b.astype(F32)).astype(BF16)
    return a, b, c


def _nsa_kernel(q_ref, kvc_ref, ks_ref, kw_ref, gate_ref, ovl_ref, exp_ref, o_ref,
                m_scr, l_scr, acc_scr, out_scr, *, seq):
    qt = NSA_Q_TILE
    rr = NSA_GROUP
    m_rows = rr * qt
    n_cmp_rows = kvc_ref.shape[2]
    win_keys = WINDOW + qt
    i = pl.program_id(2)
    t0 = i * qt

    qb = q_ref[0]
    qp = jnp.concatenate([qb[:, r * LANES:(r + 1) * LANES] for r in range(rr)], axis=0)
    tq = t0 + lax.broadcasted_iota(jnp.int32, (qt, 1), 0)
    gate = jax.nn.sigmoid(gate_ref[0])

    def emit(branch, o_norm, first):
        for r in range(rr):
            c = r * N_BRANCH + branch
            contrib = gate[:, c:c + 1] * o_norm[r * qt:(r + 1) * qt]
            if first:
                out_scr[r * qt:(r + 1) * qt, :] = contrib
            else:
                out_scr[r * qt:(r + 1) * qt, :] += contrib

    kvc = kvc_ref[0, 0]
    s = _dot_nt(qp, kvc).reshape(rr, qt, n_cmp_rows)
    n_idx = lax.broadcasted_iota(jnp.int32, (1, n_cmp_rows), 1)
    n_cmp = (seq - CMP_LEN) // CMP_STRIDE + 1
    valid = (n_idx * CMP_STRIDE + (CMP_LEN - 1) <= tq) & (n_idx < n_cmp)
    s = s + jnp.where(valid, 0.0, NEG_INF)[None]
    mx = jnp.max(s, axis=-1, keepdims=True)
    e = jnp.exp(s - mx)
    inv = 1.0 / jnp.sum(e, axis=-1, keepdims=True)
    has_cmp = (tq >= CMP_LEN - 1).astype(F32)
    p = e * (inv * has_cmp[None])
    o_cmp = _dot(p.reshape(m_rows, n_cmp_rows).astype(BF16), kvc)
    emit(0, o_cmp, True)

    psum = p[0]
    for r in range(1, rr):
        psum = psum + p[r]
    ovl = ovl_ref[...]
    imp = None
    for term in _split3(psum):
        part = _dot(term, ovl)
        imp = part if imp is None else imp + part
    n_sel = seq // SEL_BLOCK
    j_i = lax.broadcasted_iota(jnp.int32, (1, LANES), 1)
    j_f = j_i.astype(F32)
    cur = tq // SEL_BLOCK
    blk_ok = j_i * SEL_BLOCK <= tq
    forced = (j_i == 0) | (j_i == cur) | (j_i == cur - 1)
    score = jnp.where(blk_ok, imp + jnp.where(forced, FORCE_BONUS, 0.0), NEG_INF)
    score = jnp.where(j_i < n_sel, score, -jnp.inf)
    sel = jnp.zeros((qt, LANES), F32)
    for _ in range(min(SEL_TOP, n_sel)):
        best = jnp.max(score, axis=-1, keepdims=True)
        first_best = jnp.min(jnp.where(score == best, j_f, float(LANES)), axis=-1, keepdims=True)
        pick = j_f == first_best
        sel = jnp.where(pick, 1.0, sel)
        score = jnp.where(pick, -jnp.inf, score)
    sel_b = sel.astype(BF16)

    kc = SEL_KEY_CHUNK
    m_scr[...] = jnp.full(m_scr.shape, NEG_INF, F32)
    l_scr[...] = jnp.zeros(l_scr.shape, F32)
    acc_scr[...] = jnp.zeros(acc_scr.shape, F32)

    def sel_step(c, carry):
        k0 = pl.multiple_of(c * kc, kc)
        kv = ks_ref[0, pl.ds(k0, kc), :]
        s = _dot_nt(qp, kv).reshape(rr, qt, kc)
        chosen = _dot(sel_b, exp_ref[:, pl.ds(k0, kc)])
        kpos = k0 + lax.broadcasted_iota(jnp.int32, (1, kc), 1)
        ok = (chosen > 0.5) & (kpos <= tq)
        s = s + jnp.where(ok, 0.0, NEG_INF)[None]
        m_old = m_scr[...].reshape(rr, qt, 1)
        m_new = jnp.maximum(m_old, jnp.max(s, axis=-1, keepdims=True))
        alpha = jnp.exp(m_old - m_new)
        pexp = jnp.exp(s - m_new)
        l_new = alpha * l_scr[...].reshape(rr, qt, 1) + jnp.sum(pexp, axis=-1, keepdims=True)
        pv = _dot(pexp.reshape(m_rows, kc).astype(BF16), kv)
        acc_scr[...] = alpha.reshape(m_rows, 1) * acc_scr[...] + pv
        m_scr[...] = m_new.reshape(m_rows, 1)
        l_scr[...] = l_new.reshape(m_rows, 1)
        return carry

    n_steps = (t0 + qt + kc - 1) // kc
    lax.fori_loop(0, n_steps, sel_step, 0)
    emit(1, acc_scr[...] * (1.0 / l_scr[...]), False)

    w0 = pl.multiple_of(jnp.maximum(t0 - WINDOW, 0), qt)
    kv = kw_ref[0, pl.ds(w0, win_keys), :]
    s = _dot_nt(qp, kv).reshape(rr, qt, win_keys)
    wpos = w0 + lax.broadcasted_iota(jnp.int32, (1, win_keys), 1)
    ok = (wpos <= tq) & (wpos > tq - WINDOW)
    s = s + jnp.where(ok, 0.0, NEG_INF)[None]
    mx = jnp.max(s, axis=-1, keepdims=True)
    e = jnp.exp(s - mx)
    inv = 1.0 / jnp.sum(e, axis=-1, keepdims=True)
    o_win = _dot(e.reshape(m_rows, win_keys).astype(BF16), kv) * inv.reshape(m_rows, 1)
    emit(2, o_win, False)

    lane = lax.broadcasted_iota(jnp.int32, (1, LANES), 1)
    for pair in range(rr // 2):
        even = pltpu.roll(out_scr[(2 * pair) * qt:(2 * pair + 1) * qt, :], NSA_HEAD_DIM, 1)
        odd = out_scr[(2 * pair + 1) * qt:(2 * pair + 2) * qt, :]
        o_ref[0, :, pair * LANES:(pair + 1) * LANES] = jnp.where(lane < NSA_HEAD_DIM, even, odd).astype(o_ref.dtype)


def _nsa_attention(qpad, kvc, kv12, gates, batch, seq):
    qt = NSA_Q_TILE
    n_sel = seq // SEL_BLOCK
    n_chunks = seq // CMP_STRIDE
    assert n_sel <= LANES and seq % SEL_KEY_CHUNK == 0 and seq >= WINDOW + qt
    n_cmp = (seq - CMP_LEN) // CMP_STRIDE + 1
    cmp_start = np.arange(n_chunks) * CMP_STRIDE
    sel_start = np.arange(LANES) * SEL_BLOCK
    overlap = ((cmp_start[:, None] < sel_start[None, :] + SEL_BLOCK)
               & (cmp_start[:, None] + CMP_LEN > sel_start[None, :])
               & (np.arange(n_chunks)[:, None] < n_cmp) & (np.arange(LANES)[None, :] < n_sel))
    overlap = jnp.asarray(overlap.astype(np.float32), BF16)
    expand = (np.arange(seq)[None, :] // SEL_BLOCK) == np.arange(LANES)[:, None]
    expand = jnp.asarray(expand.astype(np.float32), BF16)
    gq = NSA_GROUP * LANES
    m_rows = NSA_GROUP * qt
    return pl.pallas_call(
        functools.partial(_nsa_kernel, seq=seq),
        grid=(batch, NSA_KV_HEADS, seq // qt),
        in_specs=[
            pl.BlockSpec((1, qt, gq), lambda b, g, i: (b, i, g)),
            pl.BlockSpec((1, 1, n_chunks, LANES), lambda b, g, i: (b, g, 0, 0)),
            pl.BlockSpec((1, seq, LANES), lambda b, g, i: (b, 0, g)),
            pl.BlockSpec((1, seq, LANES), lambda b, g, i: (b, 0, NSA_KV_HEADS + g)),
            pl.BlockSpec((1, qt, LANES), lambda b, g, i: (b, i, g)),
            pl.BlockSpec((n_chunks, LANES), lambda b, g, i: (0, 0)),
            pl.BlockSpec((LANES, seq), lambda b, g, i: (0, 0)),
        ],
        out_specs=pl.BlockSpec((1, qt, NSA_GROUP * NSA_HEAD_DIM), lambda b, g, i: (b, i, g)),
        out_shape=jax.ShapeDtypeStruct((batch, seq, NSA_WIDTH), BF16),
        scratch_shapes=[
            pltpu.VMEM((m_rows, 1), F32),
            pltpu.VMEM((m_rows, 1), F32),
            pltpu.VMEM((m_rows, LANES), F32),
            pltpu.VMEM((m_rows, LANES), F32),
        ],
        compiler_params=_params(3),
        name="nsa_attention",
    )(qpad, kvc, kv12, kv12, gates, overlap, expand)


def _pool_out_kernel(x_ref, nsa_ref, u_ref, uh_ref, pw_ref, ps_ref, wo_ref, g_ref, b_ref, o_ref, *, seq):
    tm = x_ref.shape[0]
    i = pl.program_id(0)
    tiles_per_seq = seq // tm
    first = (i % tiles_per_seq) == 0
    t_seq = (i % tiles_per_seq) * tm + lax.broadcasted_iota(jnp.int32, (tm, 1), 0)
    u = u_ref[...]
    halo = jnp.where(first, 0.0, uh_ref[...])
    z = jnp.concatenate([halo, u], axis=0)
    gd = POOL_GROUP_DIM
    y = jnp.dot(nsa_ref[...], wo_ref[0:NSA_WIDTH, :], preferred_element_type=F32)
    for gi, w in enumerate(POOL_WINDOWS):
        run = z[:, gi * gd:(gi + 1) * gd]
        span = 1
        while span < w:
            run = run + pltpu.roll(run, span, 0)
            span *= 2
        cnt = jnp.minimum(t_seq + 1, w).astype(F32)
        pooled = run[POOL_HALO:, :] / cnt - u[:, gi * gd:(gi + 1) * gd]
        yg = _dot(pooled.astype(BF16), pw_ref[gi]) * ps_ref[:, gi * gd:(gi + 1) * gd]
        y = y + _dot(yg.astype(BF16), wo_ref[NSA_WIDTH + gi * gd:NSA_WIDTH + (gi + 1) * gd, :])
    o_ref[...] = _layer_norm(ALPHA * x_ref[...] + y, g_ref[...], b_ref[...])


def _pool_out(x2d, nsa2d, u2d, pool_w, pool_scale, w_out, ln_g, ln_b, seq):
    t = x2d.shape[0]
    tm = ROW_TILE
    hb = tm // POOL_HALO
    return pl.pallas_call(
        functools.partial(_pool_out_kernel, seq=seq),
        grid=(t // tm,),
        in_specs=[
            pl.BlockSpec((tm, D_MODEL), lambda i: (i, 0)),
            pl.BlockSpec((tm, NSA_WIDTH), lambda i: (i, 0)),
            pl.BlockSpec((tm, POOL_WIDTH), lambda i: (i, 0)),
            pl.BlockSpec((POOL_HALO, POOL_WIDTH), lambda i: (jnp.maximum(i * hb - 1, 0), 0)),
            pl.BlockSpec(pool_w.shape, lambda i: (0, 0, 0)),
            pl.BlockSpec((1, POOL_WIDTH), lambda i: (0, 0)),
            pl.BlockSpec(w_out.shape, lambda i: (0, 0)),
            pl.BlockSpec((1, D_MODEL), lambda i: (0, 0)),
            pl.BlockSpec((1, D_MODEL), lambda i: (0, 0)),
        ],
        out_specs=pl.BlockSpec((tm, D_MODEL), lambda i: (i, 0)),
        out_shape=jax.ShapeDtypeStruct((t, D_MODEL), F32),
        compiler_params=_params(1),
        name="pool_out_ln",
    )(x2d, nsa2d, u2d, u2d, pool_w.astype(BF16), pool_scale.reshape(1, -1), w_out.astype(BF16),
      ln_g.reshape(1, -1), ln_b.reshape(1, -1))


def _sgu_kernel(x_ref, win_ref, ng_ref, nb_ref, ws_ref, bs_ref, wo_ref, g_ref, b_ref, o_ref, sv_scr):
    tm = x_ref.shape[0]
    n_chunk = tm // SG_CHUNK
    gd = SG_GROUP_DIM
    x = x_ref[...]
    xb = x.astype(BF16)
    zu = _gelu(_dot(xb, win_ref[:, 0:SG_WIDTH]))
    zv = _gelu(_dot(xb, win_ref[:, SG_WIDTH:2 * SG_WIDTH]))
    v = _layer_norm(zv, ng_ref[...], nb_ref[...]).astype(BF16)
    row = lax.broadcasted_iota(jnp.int32, (SG_CHUNK, SG_CHUNK), 0)
    col = lax.broadcasted_iota(jnp.int32, (SG_CHUNK, SG_CHUNK), 1)
    causal = col <= row
    for h in range(SG_GROUPS):
        w = jnp.where(causal, ws_ref[h], 0.0).astype(BF16)
        rhs = jnp.concatenate(
            [v[c * SG_CHUNK:(c + 1) * SG_CHUNK, h * gd:(h + 1) * gd] for c in range(n_chunk)], axis=1)
        res = _dot(w, rhs)
        for c in range(n_chunk):
            sv_scr[c * SG_CHUNK:(c + 1) * SG_CHUNK, h * gd:(h + 1) * gd] = res[:, c * gd:(c + 1) * gd]
    bias = bs_ref[...]
    for c in range(n_chunk):
        rows = slice(c * SG_CHUNK, (c + 1) * SG_CHUNK)
        sv_scr[rows, :] = zu[rows, :] * (sv_scr[rows, :] + bias)
    y = _dot(sv_scr[...].astype(BF16), wo_ref[...])
    o_ref[...] = _layer_norm(ALPHA * x + y, g_ref[...], b_ref[...])


def _sgu(x2d, w_in, norm_g, norm_b, w_s, b_s, w_out, ln_g, ln_b):
    t = x2d.shape[0]
    tm = ROW_TILE
    bias = jnp.repeat(b_s.T, SG_GROUP_DIM, axis=1)
    vec = lambda a: a.reshape(1, -1)
    const2 = lambda i: (0, 0)
    return pl.pallas_call(
        _sgu_kernel,
        grid=(t // tm,),
        in_specs=[
            pl.BlockSpec((tm, D_MODEL), lambda i: (i, 0)),
            pl.BlockSpec(w_in.shape, const2),
            pl.BlockSpec((1, SG_WIDTH), const2),
            pl.BlockSpec((1, SG_WIDTH), const2),
            pl.BlockSpec(w_s.shape, lambda i: (0, 0, 0)),
            pl.BlockSpec(bias.shape, const2),
            pl.BlockSpec(w_out.shape, const2),
            pl.BlockSpec((1, D_MODEL), const2),
            pl.BlockSpec((1, D_MODEL), const2),
        ],
        out_specs=pl.BlockSpec((tm, D_MODEL), lambda i: (i, 0)),
        out_shape=jax.ShapeDtypeStruct((t, D_MODEL), F32),
        scratch_shapes=[pltpu.VMEM((tm, SG_WIDTH), F32)],
        compiler_params=_params(1),
        name="sgu_out_ln",
    )(x2d, w_in.astype(BF16), vec(norm_g), vec(norm_b), w_s, bias, w_out.astype(BF16), vec(ln_g), vec(ln_b))


def _mem_kv_kernel(m_ref, w_ref, o_ref):
    o_ref[...] = _dot(m_ref[...].astype(BF16), w_ref[...]).astype(o_ref.dtype)


def _mem_kv(mem2d, wkv):
    rows = mem2d.shape[0]
    n = wkv.shape[1]
    tm = min(ROW_TILE, rows)
    return pl.pallas_call(
        _mem_kv_kernel,
        grid=(rows // tm,),
        in_specs=[pl.BlockSpec((tm, D_MODEL), lambda i: (i, 0)), pl.BlockSpec(wkv.shape, lambda i: (0, 0))],
        out_specs=pl.BlockSpec((tm, n), lambda i: (i, 0)),
        out_shape=jax.ShapeDtypeStruct((rows, n), BF16),
        compiler_params=_params(1),
        name="mem_kv",
    )(mem2d, wkv.astype(BF16))


def _mem_attn_kernel(x_ref, kv_ref, wq_ref, wo_ref, g_ref, b_ref, o_ref):
    x = x_ref[...]
    q = _dot(x.astype(BF16), wq_ref[...]).astype(BF16)
    hd = MEM_HEAD_DIM
    width = MEM_HEADS * hd
    outs = []
    for h in range(MEM_HEADS):
        k = kv_ref[0, :, h * hd:(h + 1) * hd]
        v = kv_ref[0, :, width + h * hd:width + (h + 1) * hd]
        s = _dot_nt(q[:, h * hd:(h + 1) * hd], k)
        e = jnp.exp(s - jnp.max(s, axis=-1, keepdims=True))
        p = e * (1.0 / jnp.sum(e, axis=-1, keepdims=True))
        outs.append(_dot(p.astype(BF16), v).astype(BF16))
    o = jnp.concatenate(outs, axis=1)
    y = _dot(o, wo_ref[...])
    o_ref[...] = _layer_norm(ALPHA * x + y, g_ref[...], b_ref[...])


def _mem_attn(x2d, memkv, wq, wo, ln_g, ln_b, seq):
    t = x2d.shape[0]
    tm = ROW_TILE
    tiles_per_seq = seq // tm
    mem_len, kv_cols = memkv.shape[1], memkv.shape[2]
    const2 = lambda i: (0, 0)
    wq_s = (wq * MEM_HEAD_DIM ** -0.5).astype(BF16)
    return pl.pallas_call(
        _mem_attn_kernel,
        grid=(t // tm,),
        in_specs=[
            pl.BlockSpec((tm, D_MODEL), lambda i: (i, 0)),
            pl.BlockSpec((1, mem_len, kv_cols), lambda i: (i // tiles_per_seq, 0, 0)),
            pl.BlockSpec(wq.shape, const2),
            pl.BlockSpec(wo.shape, const2),
            pl.BlockSpec((1, D_MODEL), const2),
            pl.BlockSpec((1, D_MODEL), const2),
        ],
        out_specs=pl.BlockSpec((tm, D_MODEL), lambda i: (i, 0)),
        out_shape=jax.ShapeDtypeStruct((t, D_MODEL), F32),
        compiler_params=_params(1),
        name="mem_attn_ln",
    )(x2d, memkv, wq_s, wo.astype(BF16), ln_g.reshape(1, -1), ln_b.reshape(1, -1))


def _ffn_kernel(x_ref, xh_ref, wup_ref, cw_ref, cb_ref, wd_ref, g_ref, b_ref, o_ref, *, seq):
    tm = x_ref.shape[0]
    i = pl.program_id(0)
    first = (i % (seq // tm)) == 0
    x = x_ref[...]
    xcat = jnp.concatenate([xh_ref[...], x], axis=0).astype(BF16)
    rows = lax.broadcasted_iota(jnp.int32, (CONV_HALO + tm, 1), 0)
    keep = jnp.where(first & (rows < CONV_HALO), 0.0, 1.0)

    def conv(col0):
        h = _dot(xcat, wup_ref[:, col0:col0 + FFN_CHUNK]) * keep
        cw = cw_ref[:, col0:col0 + FFN_CHUNK]
        out = cw[2:3, :] * h
        for back in range(1, CONV_WIDTH):
            out = out + cw[CONV_WIDTH - 1 - back:CONV_WIDTH - back, :] * pltpu.roll(h, back, 0)
        return out[CONV_HALO:, :] + cb_ref[:, col0:col0 + FFN_CHUNK]

    y = jnp.zeros((tm, D_MODEL), F32)
    for c in range(D_FF // FFN_CHUNK):
        a = conv(c * FFN_CHUNK)
        gt = conv(D_FF + c * FFN_CHUNK)
        act = (a * (gt * jax.nn.sigmoid(gt))).astype(BF16)
        y = y + _dot(act, wd_ref[c * FFN_CHUNK:(c + 1) * FFN_CHUNK, :])
    o_ref[...] = _layer_norm(ALPHA * x + y, g_ref[...], b_ref[...])


def _ffn(x2d, w_up, conv_w, conv_b, w_down, ln_g, ln_b, seq):
    t = x2d.shape[0]
    tm = ROW_TILE
    hb = tm // CONV_HALO
    const2 = lambda i: (0, 0)
    return pl.pallas_call(
        functools.partial(_ffn_kernel, seq=seq),
        grid=(t // tm,),
        in_specs=[
            pl.BlockSpec((tm, D_MODEL), lambda i: (i, 0)),
            pl.BlockSpec((CONV_HALO, D_MODEL), lambda i: (jnp.maximum(i * hb - 1, 0), 0)),
            pl.BlockSpec(w_up.shape, const2, pipeline_mode=pl.Buffered(1)),
            pl.BlockSpec(conv_w.shape, const2),
            pl.BlockSpec((1, 2 * D_FF), const2),
            pl.BlockSpec(w_down.shape, const2, pipeline_mode=pl.Buffered(1)),
            pl.BlockSpec((1, D_MODEL), const2),
            pl.BlockSpec((1, D_MODEL), const2),
        ],
        out_specs=pl.BlockSpec((tm, D_MODEL), lambda i: (i, 0)),
        out_shape=jax.ShapeDtypeStruct((t, D_MODEL), F32),
        compiler_params=_params(1),
        name="conv_ffn_ln",
    )(x2d, x2d, w_up.astype(BF16), conv_w, conv_b.reshape(1, -1), w_down.astype(BF16),
      ln_g.reshape(1, -1), ln_b.reshape(1, -1))


def kernel(x, mem, ab_w_in, nsa_cmp_pe, nsa_cmp_w1, nsa_cmp_w2, pool_w, pool_scale, ab_w_out,
           sg_w_in, sg_norm_g, sg_norm_b, sg_w_s, sg_b_s, sg_w_out, ln_g, ln_b,
           mem_wq, mem_wkv, mem_wo, ffn_w_up, ffn_conv_w, ffn_conv_b, ffn_w_down):
    batch, seq, d = x.shape
    assert d == D_MODEL and seq % ROW_TILE == 0 and seq % NSA_Q_TILE == 0
    t = batch * seq
    h = x.reshape(t, d)
    mem2d = mem.reshape(batch * mem.shape[1], d)
    for layer in range(DEPTH):
        i = layer // 2
        if layer % 2 == 0:
            qpad, kv0, kv12, gates, u = _in_proj(h, _prep_in_proj_weight(ab_w_in[i]))
            kvc = _cmp_kv(kv0, nsa_cmp_pe[i], nsa_cmp_w1[i], nsa_cmp_w2[i], batch, seq)
            nsa = _nsa_attention(qpad.reshape(batch, seq, Q_COLS), kvc, kv12.reshape(batch, seq, KV12_COLS),
                                 gates.reshape(batch, seq, GATE_COLS), batch, seq)
            h = _pool_out(h, nsa.reshape(t, NSA_WIDTH), u, pool_w[i], pool_scale[i], ab_w_out[i],
                          ln_g[layer, 0], ln_b[layer, 0], seq)
        else:
            h = _sgu(h, sg_w_in[i], sg_norm_g[i], sg_norm_b[i], sg_w_s[i], sg_b_s[i], sg_w_out[i],
                     ln_g[layer, 0], ln_b[layer, 0])
        memkv = _mem_kv(mem2d, mem_wkv[layer]).reshape(batch, mem.shape[1], -1)
        h = _mem_attn(h, memkv, mem_wq[layer], mem_wo[layer], ln_g[layer, 1], ln_b[layer, 1], seq)
        h = _ffn(h, ffn_w_up[layer], ffn_conv_w[layer], ffn_conv_b[layer], ffn_w_down[layer],
                 ln_g[layer, 2], ln_b[layer, 2], seq)
    return h.reshape(batch, seq, d)
```

```python
import functools

import numpy as np
import jax
import jax.numpy as jnp
from jax import lax
from jax.experimental import pallas as pl
from jax.experimental.pallas import tpu as pltpu

F32 = jnp.float32
BF16 = jnp.bfloat16

LANES = 128
SUBLANES = 8
VMEM_LIMIT_BYTES = 56 * 1024 * 1024

D_MODEL = 1024
DEPTH = 2
NSA_HEADS = 8
NSA_KV_HEADS = 2
NSA_GROUP = NSA_HEADS // NSA_KV_HEADS
NSA_HEAD_DIM = 64
NSA_WIDTH = NSA_HEADS * NSA_HEAD_DIM
N_BRANCH = 3
CMP_LEN = 32
CMP_STRIDE = 16
CMP_HIDDEN = 2 * NSA_HEAD_DIM
SEL_BLOCK = 64
SEL_TOP = 16
WINDOW = 512
FORCE_BONUS = 100.0
KV_WIDTH = N_BRANCH * 2 * NSA_KV_HEADS * NSA_HEAD_DIM
GATE_WIDTH = NSA_HEADS * N_BRANCH
POOL_WINDOWS = (2, 4, 8, 16)
POOL_WIDTH = D_MODEL // 2
POOL_GROUP_DIM = POOL_WIDTH // len(POOL_WINDOWS)
SG_CHUNK = 128
SG_GROUPS = 8
SG_WIDTH = D_MODEL
SG_GROUP_DIM = SG_WIDTH // SG_GROUPS
MEM_HEADS = 4
MEM_HEAD_DIM = D_MODEL // MEM_HEADS
D_FF = 256 * ((8 * D_MODEL // 3 + 255) // 256)
CONV_WIDTH = 3
ALPHA = (2 * DEPTH) ** 0.25
LN_EPS = 1e-5
NEG_INF = -1e30

ROW_TILE = 512
NSA_Q_TILE = 128
SEL_KEY_CHUNK = 512
FFN_CHUNK = 256
POOL_HALO = 16
CONV_HALO = SUBLANES


def _params(n_parallel):
    return pltpu.CompilerParams(
        dimension_semantics=("parallel",) * n_parallel,
        vmem_limit_bytes=VMEM_LIMIT_BYTES,
    )


def _layer_norm(r, g, b):
    mu = jnp.mean(r, axis=-1, keepdims=True)
    d = r - mu
    var = jnp.mean(d * d, axis=-1, keepdims=True)
    return d * lax.rsqrt(var + LN_EPS) * g + b


def _gelu(x):
    c = np.sqrt(2.0 / np.pi).astype(np.float32)
    return 0.5 * x * (1.0 + jnp.tanh(c * (x + 0.044715 * (x * x * x))))


def _dot(a, b):
    return jnp.dot(a, b, preferred_element_type=F32)


def _dot_nt(a, b):
    return lax.dot_general(a, b, (((1,), (1,)), ((), ())), preferred_element_type=F32)


LOG2E = float(np.log2(np.e))
Q_COLS = NSA_HEADS * LANES
KV0_COLS = NSA_KV_HEADS * LANES
SLAB_KSEL, SLAB_VSEL, SLAB_KWIN, SLAB_VWIN = range(4)
N_SLAB = 4
KVS_COLS = N_SLAB * NSA_KV_HEADS * LANES
GATE_COLS = NSA_KV_HEADS * LANES
IN_PROJ_SPLITS = [int(v) for v in np.cumsum([0, Q_COLS, KV0_COLS, KVS_COLS, GATE_COLS, POOL_WIDTH])]


def _in_proj_kernel(x_ref, w_ref, q_ref, kv0_ref, kvs_ref, gate_ref, u_ref, *, seq):
    tm = x_ref.shape[0]
    xb = x_ref[...].astype(BF16)

    def cols(k):
        return _dot(xb, w_ref[:, IN_PROJ_SPLITS[k]:IN_PROJ_SPLITS[k + 1]])

    q_ref[...] = (cols(0) * LOG2E).astype(q_ref.dtype)
    kv0_ref[...] = cols(1)
    gate_ref[...] = cols(3)
    u_ref[...] = cols(4)
    pos = (pl.program_id(0) % (seq // tm)) * tm + lax.broadcasted_iota(jnp.int32, (tm, 1), 0)
    lane = lax.broadcasted_iota(jnp.int32, (1, LANES), 1)
    upper = lane >= NSA_HEAD_DIM
    onehot = jnp.where(pos // SEL_BLOCK == lane - NSA_HEAD_DIM, 1.0, 0.0)
    acc = cols(2)
    for slab in range(N_SLAB):
        for g in range(NSA_KV_HEADS):
            c = slab * NSA_KV_HEADS + g
            part = acc[:, c * LANES:(c + 1) * LANES]
            if slab == SLAB_KSEL:
                part = jnp.where(upper, onehot, part)
            elif slab in (SLAB_VSEL, SLAB_VWIN):
                part = jnp.where(upper, 1.0, part)
            kvs_ref[:, c * LANES:(c + 1) * LANES] = part.astype(kvs_ref.dtype)


def _in_proj(x2d, w, seq):
    t = x2d.shape[0]
    n = w.shape[1]
    widths = [Q_COLS, KV0_COLS, KVS_COLS, GATE_COLS, POOL_WIDTH]
    dtypes = [BF16, F32, BF16, F32, F32]
    return pl.pallas_call(
        functools.partial(_in_proj_kernel, seq=seq),
        grid=(t // ROW_TILE,),
        in_specs=[
            pl.BlockSpec((ROW_TILE, D_MODEL), lambda i: (i, 0)),
            pl.BlockSpec((D_MODEL, n), lambda i: (0, 0)),
        ],
        out_specs=[pl.BlockSpec((ROW_TILE, wd), lambda i: (i, 0)) for wd in widths],
        out_shape=[jax.ShapeDtypeStruct((t, wd), dt) for wd, dt in zip(widths, dtypes)],
        compiler_params=_params(1),
        name="in_proj",
    )(x2d, w)


def _prep_in_proj_weight(w):
    d = w.shape[0]
    o_q = NSA_WIDTH
    o_kv = o_q + KV_WIDTH
    o_g = o_kv + GATE_WIDTH
    scale = NSA_HEAD_DIM ** -0.5
    wq = (w[:, :o_q] * scale).reshape(d, NSA_HEADS, NSA_HEAD_DIM)
    wq = jnp.pad(wq, ((0, 0), (0, 0), (0, LANES - NSA_HEAD_DIM))).reshape(d, Q_COLS)
    wkv = w[:, o_q:o_kv].reshape(d, N_BRANCH, 2, NSA_KV_HEADS, NSA_HEAD_DIM)
    wkv0 = wkv[:, 0].transpose(0, 2, 1, 3).reshape(d, KV0_COLS)
    wkvs = jnp.pad(wkv[:, 1:], ((0, 0),) * 4 + ((0, LANES - NSA_HEAD_DIM),)).reshape(d, KVS_COLS)
    wg = w[:, o_kv:o_g].reshape(d, NSA_KV_HEADS, NSA_GROUP * N_BRANCH)
    wg = jnp.pad(wg, ((0, 0), (0, 0), (0, LANES - NSA_GROUP * N_BRANCH))).reshape(d, GATE_COLS)
    wu = w[:, o_g:]
    return jnp.concatenate([wq, wkv0, wkvs, wg, wu], axis=1).astype(BF16)


def _cmp_kv_kernel(x_ref, pe_ref, w1_ref, w2_ref, o_ref):
    n_chunks = x_ref.shape[3]
    lane = lax.broadcasted_iota(jnp.int32, (1, LANES), 1)
    for kv in range(2):
        x = x_ref[0, 0, kv]
        lo = _dot((x + pe_ref[kv, 0:1, :]).astype(BF16), w1_ref[kv, 0])
        hi = _dot((x + pe_ref[kv, 1:2, :]).astype(BF16), w1_ref[kv, 1])
        hid = _gelu(lo + pltpu.roll(hi, n_chunks - 1, 0))
        part = _dot(hid.astype(BF16), w2_ref[kv])
        if kv == 1:
            part = jnp.where(lane >= NSA_HEAD_DIM, 1.0, part)
        o_ref[0, 0, kv] = part.astype(o_ref.dtype)


def _cmp_kv(kv0, pe, w1, w2, batch, seq):
    n_chunks = seq // CMP_STRIDE
    half = CMP_STRIDE * NSA_HEAD_DIM
    x = kv0.reshape(batch, n_chunks, CMP_STRIDE, NSA_KV_HEADS, 2, NSA_HEAD_DIM)
    x = x.transpose(0, 3, 4, 1, 2, 5).reshape(batch, NSA_KV_HEADS, 2, n_chunks, half)
    pe2 = pe.reshape(2, 2, half)
    w1b = w1.reshape(2, 2, half, CMP_HIDDEN).astype(BF16)
    w2p = jnp.pad(w2, ((0, 0), (0, 0), (0, LANES - NSA_HEAD_DIM))).astype(BF16)
    return pl.pallas_call(
        _cmp_kv_kernel,
        grid=(batch, NSA_KV_HEADS),
        in_specs=[
            pl.BlockSpec((1, 1, 2, n_chunks, half), lambda b, g: (b, g, 0, 0, 0)),
            pl.BlockSpec((2, 2, half), lambda b, g: (0, 0, 0)),
            pl.BlockSpec((2, 2, half, CMP_HIDDEN), lambda b, g: (0, 0, 0, 0)),
            pl.BlockSpec((2, CMP_HIDDEN, LANES), lambda b, g: (0, 0, 0)),
        ],
        out_specs=pl.BlockSpec((1, 1, 2, n_chunks, LANES), lambda b, g: (b, g, 0, 0, 0)),
        out_shape=jax.ShapeDtypeStruct((batch, NSA_KV_HEADS, 2, n_chunks, LANES), BF16),
        compiler_params=_params(2),
        name="cmp_kv",
    )(x, pe2, w1b, w2p)


def _split3(x):
    a = x.astype(BF16)
    r = x - a.astype(F32)
    b = r.astype(BF16)
    c = (r - b.astype(F32)).astype(BF16)
    return a, b, c


def _lane_tiles(s):
    return [s[:, i * LANES:(i + 1) * LANES] for i in range(s.shape[1] // LANES)]


def _row_max_bcast(tiles):
    mt = tiles[0]
    for t in tiles[1:]:
        mt = jnp.maximum(mt, t)
    return jnp.broadcast_to(jnp.max(mt, axis=-1, keepdims=True), mt.shape)


def _nsa_kernel(q_ref, cmp_ref, ksel_ref, vsel_ref, kwin_ref, vwin_ref, gate_ref, ovl_ref, exp_ref, o_ref,
                sc_scr, m_scr, acc_scr, out_scr, *, seq):
    qt = NSA_Q_TILE
    rr = NSA_GROUP
    dh = NSA_HEAD_DIM
    m_rows = rr * qt
    n_cmp_rows = cmp_ref.shape[3]
    n_sel = seq // SEL_BLOCK
    win_keys = WINDOW + qt
    t0 = pl.program_id(2) * qt

    qb = q_ref[0]
    qp = jnp.concatenate([qb[:, r * LANES:(r + 1) * LANES] for r in range(rr)], axis=0)
    tq = t0 + lax.broadcasted_iota(jnp.int32, (qt, 1), 0)
    lane = lax.broadcasted_iota(jnp.int32, (1, LANES), 1)
    lower = lane < dh
    gate = jax.nn.sigmoid(gate_ref[0])

    def emit(branch, o_norm, first):
        for r in range(rr):
            c = r * N_BRANCH + branch
            contrib = gate[:, c:c + 1] * o_norm[r * qt:(r + 1) * qt]
            if first:
                out_scr[r * qt:(r + 1) * qt, :] = contrib
            else:
                out_scr[r * qt:(r + 1) * qt, :] += contrib

    def normalize(acc):
        inv = jnp.where(lower, 1.0 / pltpu.roll(acc, dh, 1), 0.0)
        return acc * inv

    kcmp = cmp_ref[0, 0, 0]
    vcmp = cmp_ref[0, 0, 1]
    s = _dot_nt(qp, kcmp).reshape(rr, qt, n_cmp_rows)
    n_idx = lax.broadcasted_iota(jnp.int32, (1, n_cmp_rows), 1)
    n_cmp = (seq - CMP_LEN) // CMP_STRIDE + 1
    valid = (n_idx * CMP_STRIDE + (CMP_LEN - 1) <= tq) & (n_idx < n_cmp)
    s = s + jnp.where(valid, 0.0, NEG_INF)[None]
    e = jnp.exp2(s - jnp.max(s, axis=-1, keepdims=True))
    inv = 1.0 / jnp.sum(e, axis=-1, keepdims=True)
    has_cmp = (tq >= CMP_LEN - 1).astype(F32)
    p = e * (inv * has_cmp[None])
    emit(0, _dot(p.reshape(m_rows, n_cmp_rows).astype(BF16), vcmp), True)

    w0 = pl.multiple_of(jnp.maximum(t0 - WINDOW, 0), qt)
    kw = kwin_ref[0, pl.ds(w0, win_keys), :]
    vw = vwin_ref[0, pl.ds(w0, win_keys), :]
    wpos = w0 + lax.broadcasted_iota(jnp.int32, (1, win_keys), 1)
    okw = (wpos <= tq) & (wpos > tq - WINDOW)
    s = _dot_nt(qp, kw).reshape(rr, qt, win_keys) + jnp.where(okw, 0.0, NEG_INF)[None]
    tiles = _lane_tiles(s.reshape(m_rows, win_keys))
    m_b = _row_max_bcast(tiles)
    pw = jnp.concatenate([jnp.exp2(t - m_b) for t in tiles], axis=1).astype(BF16)
    emit(2, normalize(_dot(pw, vw)), False)

    psum = p[0]
    for r in range(1, rr):
        psum = psum + p[r]
    ovl_t = ovl_ref[...]
    imp_t = None
    for term in _split3(psum):
        part = _dot_nt(ovl_t, term)
        imp_t = part if imp_t is None else imp_t + part
    imp = imp_t[dh:, :]
    j_s = lax.broadcasted_iota(jnp.int32, (dh, 1), 0)
    tq_l = t0 + lax.broadcasted_iota(jnp.int32, (1, qt), 1)
    cur = tq_l // SEL_BLOCK
    blk_ok = j_s * SEL_BLOCK <= tq_l
    forced = (j_s == 0) | (j_s == cur) | (j_s == cur - 1)
    score = jnp.where(blk_ok, imp + jnp.where(forced, FORCE_BONUS, 0.0), NEG_INF)
    score = jnp.where(j_s < n_sel, score, -jnp.inf)
    sc_scr[...] = score
    n_grp = dh // SUBLANES
    grp = [score[g * SUBLANES:(g + 1) * SUBLANES] for g in range(n_grp)]
    rank = [jnp.zeros((SUBLANES, qt), F32) for _ in range(n_grp)]
    sub = lax.broadcasted_iota(jnp.int32, (SUBLANES, 1), 0)
    for jp in range(dh):
        row = jnp.broadcast_to(sc_scr[jp:jp + 1, :], (SUBLANES, qt))
        gj = jp // SUBLANES
        for g in range(n_grp):
            ge = jnp.where(row >= grp[g], 1.0, 0.0)
            gt = jnp.where(row > grp[g], 1.0, 0.0)
            if g > gj:
                rank[g] = rank[g] + ge
            elif g < gj:
                rank[g] = rank[g] + gt
            else:
                rank[g] = rank[g] + jnp.where(sub > jp % SUBLANES, ge, gt)
    n_top = float(min(SEL_TOP, n_sel))
    sel_t = jnp.concatenate(rank, axis=0) < n_top
    sel_t = jnp.where(sel_t & blk_ok, 1.0, 0.0)
    sel = jnp.concatenate([jnp.zeros((dh, qt), F32), sel_t], axis=0).T

    d0 = pl.multiple_of(t0, qt)
    chosen = _dot(sel.astype(BF16), exp_ref[:, pl.ds(d0, qt)])
    kpos = t0 + lax.broadcasted_iota(jnp.int32, (1, qt), 1)
    okd = (chosen > 0.5) & (kpos <= tq)
    kd = ksel_ref[0, pl.ds(d0, qt), :]
    vd = vsel_ref[0, pl.ds(d0, qt), :]
    s = (_dot_nt(qp, kd).reshape(rr, qt, qt) + jnp.where(okd, 0.0, NEG_INF)[None]).reshape(m_rows, qt)
    m_b = _row_max_bcast(_lane_tiles(s))
    m_scr[...] = m_b
    acc_scr[...] = _dot(jnp.concatenate([jnp.exp2(t - m_b) for t in _lane_tiles(s)], axis=1).astype(BF16), vd)
    blk_lane = lane - dh
    bias = jnp.where((sel > 0.5) & (blk_lane * SEL_BLOCK < t0), 0.0, NEG_INF).astype(BF16)
    q_aug = jnp.concatenate([jnp.where(lower, qp[r * qt:(r + 1) * qt], bias) for r in range(rr)], axis=0)
    kc = SEL_KEY_CHUNK

    def sel_step(c, carry):
        k0 = pl.multiple_of(c * kc, kc)
        ks = ksel_ref[0, pl.ds(k0, kc), :]
        vs = vsel_ref[0, pl.ds(k0, kc), :]
        tiles = _lane_tiles(_dot_nt(q_aug, ks))
        m_old = m_scr[...]
        m_new = jnp.maximum(m_old, _row_max_bcast(tiles))
        alpha = jnp.exp2(m_old - m_new)
        pexp = jnp.concatenate([jnp.exp2(t - m_new) for t in tiles], axis=1).astype(BF16)
        acc_scr[...] = alpha * acc_scr[...] + _dot(pexp, vs)
        m_scr[...] = m_new
        return carry

    lax.fori_loop(0, (t0 + kc - 1) // kc, sel_step, 0)
    emit(1, normalize(acc_scr[...]), False)

    for pair in range(rr // 2):
        even = out_scr[(2 * pair) * qt:(2 * pair + 1) * qt, :]
        odd = pltpu.roll(out_scr[(2 * pair + 1) * qt:(2 * pair + 2) * qt, :], dh, 1)
        o_ref[0, :, pair * LANES:(pair + 1) * LANES] = jnp.where(lower, even, odd).astype(o_ref.dtype)


def _nsa_attention(qpad, cmp_slabs, kvs, gates, batch, seq):
    qt = NSA_Q_TILE
    dh = NSA_HEAD_DIM
    n_sel = seq // SEL_BLOCK
    n_chunks = seq // CMP_STRIDE
    assert n_sel <= dh and seq % SEL_KEY_CHUNK == 0 and seq >= WINDOW + qt and SEL_KEY_CHUNK % qt == 0
    n_cmp = (seq - CMP_LEN) // CMP_STRIDE + 1
    cmp_start = np.arange(n_chunks) * CMP_STRIDE
    blk = np.arange(LANES) - dh
    overlap_t = ((cmp_start[None, :] < blk[:, None] * SEL_BLOCK + SEL_BLOCK)
                 & (cmp_start[None, :] + CMP_LEN > blk[:, None] * SEL_BLOCK)
                 & (np.arange(n_chunks)[None, :] < n_cmp) & (blk[:, None] >= 0) & (blk[:, None] < n_sel))
    overlap_t = jnp.asarray(overlap_t.astype(np.float32), BF16)
    expand = (np.arange(seq)[None, :] // SEL_BLOCK) == blk[:, None]
    expand = jnp.asarray(expand.astype(np.float32), BF16)
    gq = NSA_GROUP * LANES
    m_rows = NSA_GROUP * qt
    ng = NSA_KV_HEADS

    def slab(k):
        return pl.BlockSpec((1, seq, LANES), lambda b, g, i: (b, 0, k * ng + g))

    return pl.pallas_call(
        functools.partial(_nsa_kernel, seq=seq),
        grid=(batch, ng, seq // qt),
        in_specs=[
            pl.BlockSpec((1, qt, gq), lambda b, g, i: (b, i, g)),
            pl.BlockSpec((1, 1, 2, n_chunks, LANES), lambda b, g, i: (b, g, 0, 0, 0)),
            slab(SLAB_KSEL), slab(SLAB_VSEL), slab(SLAB_KWIN), slab(SLAB_VWIN),
            pl.BlockSpec((1, qt, LANES), lambda b, g, i: (b, i, g)),
            pl.BlockSpec((LANES, n_chunks), lambda b, g, i: (0, 0)),
            pl.BlockSpec((LANES, seq), lambda b, g, i: (0, 0)),
        ],
        out_specs=pl.BlockSpec((1, qt, NSA_GROUP * dh), lambda b, g, i: (b, i, g)),
        out_shape=jax.ShapeDtypeStruct((batch, seq, NSA_WIDTH), BF16),
        scratch_shapes=[
            pltpu.VMEM((dh, qt), F32),
            pltpu.VMEM((m_rows, LANES), F32),
            pltpu.VMEM((m_rows, LANES), F32),
            pltpu.VMEM((m_rows, LANES), F32),
        ],
        compiler_params=_params(3),
        name="nsa_attention",
    )(qpad, cmp_slabs, kvs, kvs, kvs, kvs, gates, overlap_t, expand)


def _pool_out_kernel(x_ref, nsa_ref, u_ref, uh_ref, pw_ref, ps_ref, wo_ref, g_ref, b_ref, o_ref, *, seq):
    tm = x_ref.shape[0]
    i = pl.program_id(0)
    tiles_per_seq = seq // tm
    first = (i % tiles_per_seq) == 0
    t_seq = (i % tiles_per_seq) * tm + lax.broadcasted_iota(jnp.int32, (tm, 1), 0)
    u = u_ref[...]
    halo = jnp.where(first, 0.0, uh_ref[...])
    z = jnp.concatenate([halo, u], axis=0)
    gd = POOL_GROUP_DIM
    y = jnp.dot(nsa_ref[...], wo_ref[0:NSA_WIDTH, :], preferred_element_type=F32)
    for gi, w in enumerate(POOL_WINDOWS):
        run = z[:, gi * gd:(gi + 1) * gd]
        span = 1
        while span < w:
            run = run + pltpu.roll(run, span, 0)
            span *= 2
        cnt = jnp.minimum(t_seq + 1, w).astype(F32)
        pooled = run[POOL_HALO:, :] / cnt - u[:, gi * gd:(gi + 1) * gd]
        yg = _dot(pooled.astype(BF16), pw_ref[gi]) * ps_ref[:, gi * gd:(gi + 1) * gd]
        y = y + _dot(yg.astype(BF16), wo_ref[NSA_WIDTH + gi * gd:NSA_WIDTH + (gi + 1) * gd, :])
    o_ref[...] = _layer_norm(ALPHA * x_ref[...] + y, g_ref[...], b_ref[...])


def _pool_out(x2d, nsa2d, u2d, pool_w, pool_scale, w_out, ln_g, ln_b, seq):
    t = x2d.shape[0]
    tm = ROW_TILE
    hb = tm // POOL_HALO
    return pl.pallas_call(
        functools.partial(_pool_out_kernel, seq=seq),
        grid=(t // tm,),
        in_specs=[
            pl.BlockSpec((tm, D_MODEL), lambda i: (i, 0)),
            pl.BlockSpec((tm, NSA_WIDTH), lambda i: (i, 0)),
            pl.BlockSpec((tm, POOL_WIDTH), lambda i: (i, 0)),
            pl.BlockSpec((POOL_HALO, POOL_WIDTH), lambda i: (jnp.maximum(i * hb - 1, 0), 0)),
            pl.BlockSpec(pool_w.shape, lambda i: (0, 0, 0)),
            pl.BlockSpec((1, POOL_WIDTH), lambda i: (0, 0)),
            pl.BlockSpec(w_out.shape, lambda i: (0, 0)),
            pl.BlockSpec((1, D_MODEL), lambda i: (0, 0)),
            pl.BlockSpec((1, D_MODEL), lambda i: (0, 0)),
        ],
        out_specs=pl.BlockSpec((tm, D_MODEL), lambda i: (i, 0)),
        out_shape=jax.ShapeDtypeStruct((t, D_MODEL), F32),
        compiler_params=_params(1),
        name="pool_out_ln",
    )(x2d, nsa2d, u2d, u2d, pool_w.astype(BF16), pool_scale.reshape(1, -1), w_out.astype(BF16),
      ln_g.reshape(1, -1), ln_b.reshape(1, -1))


def _sgu_kernel(x_ref, win_ref, ng_ref, nb_ref, ws_ref, bs_ref, wo_ref, g_ref, b_ref, o_ref, sv_scr):
    tm = x_ref.shape[0]
    n_chunk = tm // SG_CHUNK
    gd = SG_GROUP_DIM
    x = x_ref[...]
    xb = x.astype(BF16)
    zu = _gelu(_dot(xb, win_ref[:, 0:SG_WIDTH]))
    zv = _gelu(_dot(xb, win_ref[:, SG_WIDTH:2 * SG_WIDTH]))
    v = _layer_norm(zv, ng_ref[...], nb_ref[...]).astype(BF16)
    row = lax.broadcasted_iota(jnp.int32, (SG_CHUNK, SG_CHUNK), 0)
    col = lax.broadcasted_iota(jnp.int32, (SG_CHUNK, SG_CHUNK), 1)
    causal = col <= row
    for h in range(SG_GROUPS):
        w = jnp.where(causal, ws_ref[h], 0.0).astype(BF16)
        rhs = jnp.concatenate(
            [v[c * SG_CHUNK:(c + 1) * SG_CHUNK, h * gd:(h + 1) * gd] for c in range(n_chunk)], axis=1)
        res = _dot(w, rhs)
        for c in range(n_chunk):
            sv_scr[c * SG_CHUNK:(c + 1) * SG_CHUNK, h * gd:(h + 1) * gd] = res[:, c * gd:(c + 1) * gd]
    bias = bs_ref[...]
    for c in range(n_chunk):
        rows = slice(c * SG_CHUNK, (c + 1) * SG_CHUNK)
        sv_scr[rows, :] = zu[rows, :] * (sv_scr[rows, :] + bias)
    y = _dot(sv_scr[...].astype(BF16), wo_ref[...])
    o_ref[...] = _layer_norm(ALPHA * x + y, g_ref[...], b_ref[...])


def _sgu(x2d, w_in, norm_g, norm_b, w_s, b_s, w_out, ln_g, ln_b):
    t = x2d.shape[0]
    tm = ROW_TILE
    bias = jnp.repeat(b_s.T, SG_GROUP_DIM, axis=1)
    vec = lambda a: a.reshape(1, -1)
    const2 = lambda i: (0, 0)
    return pl.pallas_call(
        _sgu_kernel,
        grid=(t // tm,),
        in_specs=[
            pl.BlockSpec((tm, D_MODEL), lambda i: (i, 0)),
            pl.BlockSpec(w_in.shape, const2),
            pl.BlockSpec((1, SG_WIDTH), const2),
            pl.BlockSpec((1, SG_WIDTH), const2),
            pl.BlockSpec(w_s.shape, lambda i: (0, 0, 0)),
            pl.BlockSpec(bias.shape, const2),
            pl.BlockSpec(w_out.shape, const2),
            pl.BlockSpec((1, D_MODEL), const2),
            pl.BlockSpec((1, D_MODEL), const2),
        ],
        out_specs=pl.BlockSpec((tm, D_MODEL), lambda i: (i, 0)),
        out_shape=jax.ShapeDtypeStruct((t, D_MODEL), F32),
        scratch_shapes=[pltpu.VMEM((tm, SG_WIDTH), F32)],
        compiler_params=_params(1),
        name="sgu_out_ln",
    )(x2d, w_in.astype(BF16), vec(norm_g), vec(norm_b), w_s, bias, w_out.astype(BF16), vec(ln_g), vec(ln_b))


def _mem_kv_kernel(m_ref, w_ref, o_ref):
    o_ref[...] = _dot(m_ref[...].astype(BF16), w_ref[...]).astype(o_ref.dtype)


def _mem_kv(mem2d, wkv):
    rows = mem2d.shape[0]
    n = wkv.shape[1]
    tm = min(ROW_TILE, rows)
    return pl.pallas_call(
        _mem_kv_kernel,
        grid=(rows // tm,),
        in_specs=[pl.BlockSpec((tm, D_MODEL), lambda i: (i, 0)), pl.BlockSpec(wkv.shape, lambda i: (0, 0))],
        out_specs=pl.BlockSpec((tm, n), lambda i: (i, 0)),
        out_shape=jax.ShapeDtypeStruct((rows, n), BF16),
        compiler_params=_params(1),
        name="mem_kv",
    )(mem2d, wkv.astype(BF16))


def _mem_attn_kernel(x_ref, kv_ref, wq_ref, wo_ref, g_ref, b_ref, o_ref):
    x = x_ref[...]
    q = _dot(x.astype(BF16), wq_ref[...]).astype(BF16)
    hd = MEM_HEAD_DIM
    width = MEM_HEADS * hd
    outs = []
    for h in range(MEM_HEADS):
        k = kv_ref[0, :, h * hd:(h + 1) * hd]
        v = kv_ref[0, :, width + h * hd:width + (h + 1) * hd]
        s = _dot_nt(q[:, h * hd:(h + 1) * hd], k)
        e = jnp.exp(s - jnp.max(s, axis=-1, keepdims=True))
        p = e * (1.0 / jnp.sum(e, axis=-1, keepdims=True))
        outs.append(_dot(p.astype(BF16), v).astype(BF16))
    o = jnp.concatenate(outs, axis=1)
    y = _dot(o, wo_ref[...])
    o_ref[...] = _layer_norm(ALPHA * x + y, g_ref[...], b_ref[...])


def _mem_attn(x2d, memkv, wq, wo, ln_g, ln_b, seq):
    t = x2d.shape[0]
    tm = ROW_TILE
    tiles_per_seq = seq // tm
    mem_len, kv_cols = memkv.shape[1], memkv.shape[2]
    const2 = lambda i: (0, 0)
    wq_s = (wq * MEM_HEAD_DIM ** -0.5).astype(BF16)
    return pl.pallas_call(
        _mem_attn_kernel,
        grid=(t // tm,),
        in_specs=[
            pl.BlockSpec((tm, D_MODEL), lambda i: (i, 0)),
            pl.BlockSpec((1, mem_len, kv_cols), lambda i: (i // tiles_per_seq, 0, 0)),
            pl.BlockSpec(wq.shape, const2),
            pl.BlockSpec(wo.shape, const2),
            pl.BlockSpec((1, D_MODEL), const2),
            pl.BlockSpec((1, D_MODEL), const2),
        ],
        out_specs=pl.BlockSpec((tm, D_MODEL), lambda i: (i, 0)),
        out_shape=jax.ShapeDtypeStruct((t, D_MODEL), F32),
        compiler_params=_params(1),
        name="mem_attn_ln",
    )(x2d, memkv, wq_s, wo.astype(BF16), ln_g.reshape(1, -1), ln_b.reshape(1, -1))


def _ffn_kernel(x_ref, xh_ref, wup_ref, cw_ref, cb_ref, wd_ref, g_ref, b_ref, o_ref, *, seq):
    tm = x_ref.shape[0]
    i = pl.program_id(0)
    first = (i % (seq // tm)) == 0
    x = x_ref[...]
    xcat = jnp.concatenate([xh_ref[...], x], axis=0).astype(BF16)
    rows = lax.broadcasted_iota(jnp.int32, (CONV_HALO + tm, 1), 0)
    keep = jnp.where(first & (rows < CONV_HALO), 0.0, 1.0)

    def conv(col0):
        h = _dot(xcat, wup_ref[:, col0:col0 + FFN_CHUNK]) * keep
        cw = cw_ref[:, col0:col0 + FFN_CHUNK]
        out = cw[2:3, :] * h
        for back in range(1, CONV_WIDTH):
            out = out + cw[CONV_WIDTH - 1 - back:CONV_WIDTH - back, :] * pltpu.roll(h, back, 0)
        return out[CONV_HALO:, :] + cb_ref[:, col0:col0 + FFN_CHUNK]

    y = jnp.zeros((tm, D_MODEL), F32)
    for c in range(D_FF // FFN_CHUNK):
        a = conv(c * FFN_CHUNK)
        gt = conv(D_FF + c * FFN_CHUNK)
        act = (a * (gt * jax.nn.sigmoid(gt))).astype(BF16)
        y = y + _dot(act, wd_ref[c * FFN_CHUNK:(c + 1) * FFN_CHUNK, :])
    o_ref[...] = _layer_norm(ALPHA * x + y, g_ref[...], b_ref[...])


def _ffn(x2d, w_up, conv_w, conv_b, w_down, ln_g, ln_b, seq):
    t = x2d.shape[0]
    tm = ROW_TILE
    hb = tm // CONV_HALO
    const2 = lambda i: (0, 0)
    return pl.pallas_call(
        functools.partial(_ffn_kernel, seq=seq),
        grid=(t // tm,),
        in_specs=[
            pl.BlockSpec((tm, D_MODEL), lambda i: (i, 0)),
            pl.BlockSpec((CONV_HALO, D_MODEL), lambda i: (jnp.maximum(i * hb - 1, 0), 0)),
            pl.BlockSpec(w_up.shape, const2, pipeline_mode=pl.Buffered(1)),
            pl.BlockSpec(conv_w.shape, const2),
            pl.BlockSpec((1, 2 * D_FF), const2),
            pl.BlockSpec(w_down.shape, const2, pipeline_mode=pl.Buffered(1)),
            pl.BlockSpec((1, D_MODEL), const2),
            pl.BlockSpec((1, D_MODEL), const2),
        ],
        out_specs=pl.BlockSpec((tm, D_MODEL), lambda i: (i, 0)),
        out_shape=jax.ShapeDtypeStruct((t, D_MODEL), F32),
        compiler_params=_params(1),
        name="conv_ffn_ln",
    )(x2d, x2d, w_up.astype(BF16), conv_w, conv_b.reshape(1, -1), w_down.astype(BF16),
      ln_g.reshape(1, -1), ln_b.reshape(1, -1))


def kernel(x, mem, ab_w_in, nsa_cmp_pe, nsa_cmp_w1, nsa_cmp_w2, pool_w, pool_scale, ab_w_out,
           sg_w_in, sg_norm_g, sg_norm_b, sg_w_s, sg_b_s, sg_w_out, ln_g, ln_b,
           mem_wq, mem_wkv, mem_wo, ffn_w_up, ffn_conv_w, ffn_conv_b, ffn_w_down):
    batch, seq, d = x.shape
    assert d == D_MODEL and seq % ROW_TILE == 0 and seq % NSA_Q_TILE == 0
    t = batch * seq
    h = x.reshape(t, d)
    mem2d = mem.reshape(batch * mem.shape[1], d)
    for layer in range(DEPTH):
        i = layer // 2
        if layer % 2 == 0:
            qpad, kv0, kvs, gates, u = _in_proj(h, _prep_in_proj_weight(ab_w_in[i]), seq)
            kvc = _cmp_kv(kv0, nsa_cmp_pe[i], nsa_cmp_w1[i], nsa_cmp_w2[i], batch, seq)
            nsa = _nsa_attention(qpad.reshape(batch, seq, Q_COLS), kvc, kvs.reshape(batch, seq, KVS_COLS),
                                 gates.reshape(batch, seq, GATE_COLS), batch, seq)
            h = _pool_out(h, nsa.reshape(t, NSA_WIDTH), u, pool_w[i], pool_scale[i], ab_w_out[i],
                          ln_g[layer, 0], ln_b[layer, 0], seq)
        else:
            h = _sgu(h, sg_w_in[i], sg_norm_g[i], sg_norm_b[i], sg_w_s[i], sg_b_s[i], sg_w_out[i],
                     ln_g[layer, 0], ln_b[layer, 0])
        memkv = _mem_kv(mem2d, mem_wkv[layer]).reshape(batch, mem.shape[1], -1)
        h = _mem_attn(h, memkv, mem_wq[layer], mem_wo[layer], ln_g[layer, 1], ln_b[layer, 1], seq)
        h = _ffn(h, ffn_w_up[layer], ffn_conv_w[layer], ffn_conv_b[layer], ffn_w_down[layer],
                 ln_g[layer, 2], ln_b[layer, 2], seq)
    return h.reshape(batch, seq, d)
```

```python
import functools

import numpy as np
import jax
import jax.numpy as jnp
from jax import lax
from jax.experimental import pallas as pl
from jax.experimental.pallas import tpu as pltpu

F32 = jnp.float32
BF16 = jnp.bfloat16

LANES = 128
SUBLANES = 8
VMEM_LIMIT_BYTES = 56 * 1024 * 1024

D_MODEL = 1024
DEPTH = 2
NSA_HEADS = 8
NSA_KV_HEADS = 2
NSA_GROUP = NSA_HEADS // NSA_KV_HEADS
NSA_HEAD_DIM = 64
NSA_WIDTH = NSA_HEADS * NSA_HEAD_DIM
N_BRANCH = 3
CMP_LEN = 32
CMP_STRIDE = 16
CMP_HIDDEN = 2 * NSA_HEAD_DIM
SEL_BLOCK = 64
SEL_TOP = 16
WINDOW = 512
FORCE_BONUS = 100.0
KV_WIDTH = N_BRANCH * 2 * NSA_KV_HEADS * NSA_HEAD_DIM
GATE_WIDTH = NSA_HEADS * N_BRANCH
POOL_WINDOWS = (2, 4, 8, 16)
POOL_WIDTH = D_MODEL // 2
POOL_GROUP_DIM = POOL_WIDTH // len(POOL_WINDOWS)
SG_CHUNK = 128
SG_GROUPS = 8
SG_WIDTH = D_MODEL
SG_GROUP_DIM = SG_WIDTH // SG_GROUPS
MEM_HEADS = 4
MEM_HEAD_DIM = D_MODEL // MEM_HEADS
D_FF = 256 * ((8 * D_MODEL // 3 + 255) // 256)
CONV_WIDTH = 3
ALPHA = (2 * DEPTH) ** 0.25
LN_EPS = 1e-5
NEG_INF = -1e30

ROW_TILE = 512
FFN_ROW_TILE = 1024
NSA_Q_TILE = 128
SEL_KEY_CHUNK = 512
FFN_CHUNK = 256
FFN_DOWN_GROUP = 4
POOL_HALO = 16
CONV_HALO = SUBLANES


def _params(n_parallel):
    return pltpu.CompilerParams(
        dimension_semantics=("parallel",) * n_parallel,
        vmem_limit_bytes=VMEM_LIMIT_BYTES,
    )


def _layer_norm(r, g, b):
    mu = jnp.mean(r, axis=-1, keepdims=True)
    d = r - mu
    var = jnp.mean(d * d, axis=-1, keepdims=True)
    return d * lax.rsqrt(var + LN_EPS) * g + b


def _gelu(x):
    c = np.sqrt(2.0 / np.pi).astype(np.float32)
    return 0.5 * x * (1.0 + jnp.tanh(c * (x + 0.044715 * (x * x * x))))


def _dot(a, b):
    return jnp.dot(a, b, preferred_element_type=F32)


def _dot_nt(a, b):
    return lax.dot_general(a, b, (((1,), (1,)), ((), ())), preferred_element_type=F32)


LOG2E = float(np.log2(np.e))
Q_COLS = NSA_HEADS * LANES
KV0_COLS = NSA_KV_HEADS * LANES
SLAB_KSEL, SLAB_VSEL, SLAB_KWIN, SLAB_VWIN = range(4)
N_SLAB = 4
KVS_COLS = N_SLAB * NSA_KV_HEADS * LANES
GATE_COLS = NSA_KV_HEADS * LANES
IN_PROJ_SPLITS = [int(v) for v in np.cumsum([0, Q_COLS, KV0_COLS, KVS_COLS, GATE_COLS, POOL_WIDTH])]


def _in_proj_kernel(x_ref, w_ref, q_ref, kv0_ref, kvs_ref, gate_ref, u_ref, *, seq):
    tm = x_ref.shape[0]
    xb = x_ref[...].astype(BF16)

    def cols(k):
        return _dot(xb, w_ref[:, IN_PROJ_SPLITS[k]:IN_PROJ_SPLITS[k + 1]])

    q_ref[...] = (cols(0) * LOG2E).astype(q_ref.dtype)
    kv0_ref[...] = cols(1)
    gate_ref[...] = cols(3)
    u_ref[...] = cols(4)
    pos = (pl.program_id(0) % (seq // tm)) * tm + lax.broadcasted_iota(jnp.int32, (tm, 1), 0)
    lane = lax.broadcasted_iota(jnp.int32, (1, LANES), 1)
    upper = lane >= NSA_HEAD_DIM
    onehot = jnp.where(pos // SEL_BLOCK == lane - NSA_HEAD_DIM, 1.0, 0.0)
    acc = cols(2)
    for slab in range(N_SLAB):
        for g in range(NSA_KV_HEADS):
            c = slab * NSA_KV_HEADS + g
            part = acc[:, c * LANES:(c + 1) * LANES]
            if slab == SLAB_KSEL:
                part = jnp.where(upper, onehot, part)
            elif slab in (SLAB_VSEL, SLAB_VWIN):
                part = jnp.where(upper, 1.0, part)
            kvs_ref[:, c * LANES:(c + 1) * LANES] = part.astype(kvs_ref.dtype)


def _in_proj(x2d, w, seq):
    t = x2d.shape[0]
    n = w.shape[1]
    widths = [Q_COLS, KV0_COLS, KVS_COLS, GATE_COLS, POOL_WIDTH]
    dtypes = [BF16, F32, BF16, F32, F32]
    return pl.pallas_call(
        functools.partial(_in_proj_kernel, seq=seq),
        grid=(t // ROW_TILE,),
        in_specs=[
            pl.BlockSpec((ROW_TILE, D_MODEL), lambda i: (i, 0)),
            pl.BlockSpec((D_MODEL, n), lambda i: (0, 0)),
        ],
        out_specs=[pl.BlockSpec((ROW_TILE, wd), lambda i: (i, 0)) for wd in widths],
        out_shape=[jax.ShapeDtypeStruct((t, wd), dt) for wd, dt in zip(widths, dtypes)],
        compiler_params=_params(1),
        name="in_proj",
    )(x2d, w)


def _prep_in_proj_weight(w):
    d = w.shape[0]
    o_q = NSA_WIDTH
    o_kv = o_q + KV_WIDTH
    o_g = o_kv + GATE_WIDTH
    scale = NSA_HEAD_DIM ** -0.5
    wq = (w[:, :o_q] * scale).reshape(d, NSA_HEADS, NSA_HEAD_DIM)
    wq = jnp.pad(wq, ((0, 0), (0, 0), (0, LANES - NSA_HEAD_DIM))).reshape(d, Q_COLS)
    wkv = w[:, o_q:o_kv].reshape(d, N_BRANCH, 2, NSA_KV_HEADS, NSA_HEAD_DIM)
    wkv0 = wkv[:, 0].transpose(0, 2, 1, 3).reshape(d, KV0_COLS)
    wkvs = jnp.pad(wkv[:, 1:], ((0, 0),) * 4 + ((0, LANES - NSA_HEAD_DIM),)).reshape(d, KVS_COLS)
    wg = w[:, o_kv:o_g].reshape(d, NSA_KV_HEADS, NSA_GROUP * N_BRANCH)
    wg = jnp.pad(wg, ((0, 0), (0, 0), (0, LANES - NSA_GROUP * N_BRANCH))).reshape(d, GATE_COLS)
    wu = w[:, o_g:]
    return jnp.concatenate([wq, wkv0, wkvs, wg, wu], axis=1).astype(BF16)


def _cmp_kv_kernel(x_ref, pe_ref, w1_ref, w2_ref, o_ref):
    n_chunks = x_ref.shape[3]
    lane = lax.broadcasted_iota(jnp.int32, (1, LANES), 1)
    for kv in range(2):
        x = x_ref[0, 0, kv]
        lo = _dot((x + pe_ref[kv, 0:1, :]).astype(BF16), w1_ref[kv, 0])
        hi = _dot((x + pe_ref[kv, 1:2, :]).astype(BF16), w1_ref[kv, 1])
        hid = _gelu(lo + pltpu.roll(hi, n_chunks - 1, 0))
        part = _dot(hid.astype(BF16), w2_ref[kv])
        if kv == 1:
            part = jnp.where(lane >= NSA_HEAD_DIM, 1.0, part)
        o_ref[0, 0, kv] = part.astype(o_ref.dtype)


def _cmp_kv(kv0, pe, w1, w2, batch, seq):
    n_chunks = seq // CMP_STRIDE
    half = CMP_STRIDE * NSA_HEAD_DIM
    x = kv0.reshape(batch, n_chunks, CMP_STRIDE, NSA_KV_HEADS, 2, NSA_HEAD_DIM)
    x = x.transpose(0, 3, 4, 1, 2, 5).reshape(batch, NSA_KV_HEADS, 2, n_chunks, half)
    pe2 = pe.reshape(2, 2, half)
    w1b = w1.reshape(2, 2, half, CMP_HIDDEN).astype(BF16)
    w2p = jnp.pad(w2, ((0, 0), (0, 0), (0, LANES - NSA_HEAD_DIM))).astype(BF16)
    return pl.pallas_call(
        _cmp_kv_kernel,
        grid=(batch, NSA_KV_HEADS),
        in_specs=[
            pl.BlockSpec((1, 1, 2, n_chunks, half), lambda b, g: (b, g, 0, 0, 0)),
            pl.BlockSpec((2, 2, half), lambda b, g: (0, 0, 0)),
            pl.BlockSpec((2, 2, half, CMP_HIDDEN), lambda b, g: (0, 0, 0, 0)),
            pl.BlockSpec((2, CMP_HIDDEN, LANES), lambda b, g: (0, 0, 0)),
        ],
        out_specs=pl.BlockSpec((1, 1, 2, n_chunks, LANES), lambda b, g: (b, g, 0, 0, 0)),
        out_shape=jax.ShapeDtypeStruct((batch, NSA_KV_HEADS, 2, n_chunks, LANES), BF16),
        compiler_params=_params(2),
        name="cmp_kv",
    )(x, pe2, w1b, w2p)


def _split3(x):
    a = x.astype(BF16)
    r = x - a.astype(F32)
    b = r.astype(BF16)
    c = (r - b.astype(F32)).astype(BF16)
    return a, b, c


def _lane_tiles(s):
    return [s[:, i * LANES:(i + 1) * LANES] for i in range(s.shape[1] // LANES)]


def _row_max_bcast(tiles):
    mt = tiles[0]
    for t in tiles[1:]:
        mt = jnp.maximum(mt, t)
    return jnp.broadcast_to(jnp.max(mt, axis=-1, keepdims=True), mt.shape)


def _nsa_group(g, t0, q_ref, cmp_ref, kvs_ref, gate_ref, ovl_ref, exp_ref, o_ref,
               sc_scr, m_scr, acc_scr, out_scr, seq):
    qt = NSA_Q_TILE
    rr = NSA_GROUP
    dh = NSA_HEAD_DIM
    m_rows = rr * qt
    n_cmp_rows = cmp_ref.shape[3]
    n_sel = seq // SEL_BLOCK
    win_keys = WINDOW + qt

    def slab(k):
        c0 = (k * NSA_KV_HEADS + g) * LANES
        return kvs_ref.at[:, :, c0:c0 + LANES]

    ksel_ref, vsel_ref, kwin_ref, vwin_ref = (slab(k) for k in (SLAB_KSEL, SLAB_VSEL, SLAB_KWIN, SLAB_VWIN))
    q_ref = q_ref.at[:, :, g * rr * LANES:(g + 1) * rr * LANES]
    cmp_ref = cmp_ref.at[:, g:g + 1]
    gate_ref = gate_ref.at[:, :, g * LANES:(g + 1) * LANES]
    o_ref = o_ref.at[:, :, g * rr * dh:(g + 1) * rr * dh]
    sc_scr, m_scr, acc_scr, out_scr = (r.at[g] for r in (sc_scr, m_scr, acc_scr, out_scr))

    qb = q_ref[0]
    qp = jnp.concatenate([qb[:, r * LANES:(r + 1) * LANES] for r in range(rr)], axis=0)
    tq = t0 + lax.broadcasted_iota(jnp.int32, (qt, 1), 0)
    lane = lax.broadcasted_iota(jnp.int32, (1, LANES), 1)
    lower = lane < dh
    gate = jax.nn.sigmoid(gate_ref[0])

    def emit(branch, o_norm, first):
        for r in range(rr):
            c = r * N_BRANCH + branch
            contrib = gate[:, c:c + 1] * o_norm[r * qt:(r + 1) * qt]
            if first:
                out_scr[r * qt:(r + 1) * qt, :] = contrib
            else:
                out_scr[r * qt:(r + 1) * qt, :] += contrib

    def normalize(acc):
        inv = jnp.where(lower, 1.0 / pltpu.roll(acc, dh, 1), 0.0)
        return acc * inv

    kcmp = cmp_ref[0, 0, 0]
    vcmp = cmp_ref[0, 0, 1]
    s = _dot_nt(qp, kcmp).reshape(rr, qt, n_cmp_rows)
    n_idx = lax.broadcasted_iota(jnp.int32, (1, n_cmp_rows), 1)
    n_cmp = (seq - CMP_LEN) // CMP_STRIDE + 1
    valid = (n_idx * CMP_STRIDE + (CMP_LEN - 1) <= tq) & (n_idx < n_cmp)
    s = s + jnp.where(valid, 0.0, NEG_INF)[None]
    e = jnp.exp2(s - jnp.max(s, axis=-1, keepdims=True))
    inv = 1.0 / jnp.sum(e, axis=-1, keepdims=True)
    has_cmp = (tq >= CMP_LEN - 1).astype(F32)
    p = e * (inv * has_cmp[None])
    emit(0, _dot(p.reshape(m_rows, n_cmp_rows).astype(BF16), vcmp), True)
    yield

    w0 = pl.multiple_of(jnp.maximum(t0 - WINDOW, 0), qt)
    kw = kwin_ref[0, pl.ds(w0, win_keys), :]
    vw = vwin_ref[0, pl.ds(w0, win_keys), :]
    wpos = w0 + lax.broadcasted_iota(jnp.int32, (1, win_keys), 1)
    okw = (wpos <= tq) & (wpos > tq - WINDOW)
    s = _dot_nt(qp, kw).reshape(rr, qt, win_keys) + jnp.where(okw, 0.0, NEG_INF)[None]
    tiles = _lane_tiles(s.reshape(m_rows, win_keys))
    m_b = _row_max_bcast(tiles)
    pw = jnp.concatenate([jnp.exp2(t - m_b) for t in tiles], axis=1).astype(BF16)
    emit(2, normalize(_dot(pw, vw)), False)
    yield

    psum = p[0]
    for r in range(1, rr):
        psum = psum + p[r]
    ovl_t = ovl_ref[...]
    imp_t = None
    for term in _split3(psum):
        part = _dot_nt(ovl_t, term)
        imp_t = part if imp_t is None else imp_t + part
    imp = imp_t[dh:, :]
    j_s = lax.broadcasted_iota(jnp.int32, (dh, 1), 0)
    tq_l = t0 + lax.broadcasted_iota(jnp.int32, (1, qt), 1)
    cur = tq_l // SEL_BLOCK
    blk_ok = j_s * SEL_BLOCK <= tq_l
    forced = (j_s == 0) | (j_s == cur) | (j_s == cur - 1)
    score = jnp.where(blk_ok, imp + jnp.where(forced, FORCE_BONUS, 0.0), NEG_INF)
    score = jnp.where(j_s < n_sel, score, -jnp.inf)
    sc_scr[...] = score
    yield
    n_grp = dh // SUBLANES
    grp = [score[k * SUBLANES:(k + 1) * SUBLANES] for k in range(n_grp)]
    rank = [jnp.zeros((SUBLANES, qt), F32) for _ in range(n_grp)]
    sub = lax.broadcasted_iota(jnp.int32, (SUBLANES, 1), 0)
    for jp in range(dh):
        row = jnp.broadcast_to(sc_scr[jp:jp + 1, :], (SUBLANES, qt))
        kj = jp // SUBLANES
        for k in range(n_grp):
            ge = jnp.where(row >= grp[k], 1.0, 0.0)
            gt = jnp.where(row > grp[k], 1.0, 0.0)
            if k > kj:
                rank[k] = rank[k] + ge
            elif k < kj:
                rank[k] = rank[k] + gt
            else:
                rank[k] = rank[k] + jnp.where(sub > jp % SUBLANES, ge, gt)
    n_top = float(min(SEL_TOP, n_sel))
    sel_t = jnp.concatenate(rank, axis=0) < n_top
    sel_t = jnp.where(sel_t & blk_ok, 1.0, 0.0)
    sel = jnp.concatenate([jnp.zeros((dh, qt), F32), sel_t], axis=0).T
    yield

    d0 = pl.multiple_of(t0, qt)
    chosen = _dot(sel.astype(BF16), exp_ref[:, pl.ds(d0, qt)])
    kpos = t0 + lax.broadcasted_iota(jnp.int32, (1, qt), 1)
    okd = (chosen > 0.5) & (kpos <= tq)
    kd = ksel_ref[0, pl.ds(d0, qt), :]
    vd = vsel_ref[0, pl.ds(d0, qt), :]
    s = (_dot_nt(qp, kd).reshape(rr, qt, qt) + jnp.where(okd, 0.0, NEG_INF)[None]).reshape(m_rows, qt)
    m_b = _row_max_bcast(_lane_tiles(s))
    m_scr[...] = m_b
    acc_scr[...] = _dot(jnp.concatenate([jnp.exp2(t - m_b) for t in _lane_tiles(s)], axis=1).astype(BF16), vd)
    blk_lane = lane - dh
    bias = jnp.where((sel > 0.5) & (blk_lane * SEL_BLOCK < t0), 0.0, NEG_INF).astype(BF16)
    q_aug = jnp.concatenate([jnp.where(lower, qp[r * qt:(r + 1) * qt], bias) for r in range(rr)], axis=0)
    kc = SEL_KEY_CHUNK

    def step(c):
        k0 = pl.multiple_of(c * kc, kc)
        ks = ksel_ref[0, pl.ds(k0, kc), :]
        vs = vsel_ref[0, pl.ds(k0, kc), :]
        tiles = _lane_tiles(_dot_nt(q_aug, ks))
        m_old = m_scr[...]
        m_new = jnp.maximum(m_old, _row_max_bcast(tiles))
        alpha = jnp.exp2(m_old - m_new)
        pexp = jnp.concatenate([jnp.exp2(t - m_new) for t in tiles], axis=1).astype(BF16)
        acc_scr[...] = alpha * acc_scr[...] + _dot(pexp, vs)
        m_scr[...] = m_new

    def finish():
        emit(1, normalize(acc_scr[...]), False)
        for pair in range(rr // 2):
            even = out_scr[(2 * pair) * qt:(2 * pair + 1) * qt, :]
            odd = pltpu.roll(out_scr[(2 * pair + 1) * qt:(2 * pair + 2) * qt, :], dh, 1)
            o_ref[0, :, pair * LANES:(pair + 1) * LANES] = jnp.where(lower, even, odd).astype(o_ref.dtype)

    return step, finish


def _nsa_kernel(*refs, seq):
    t0 = pl.program_id(1) * NSA_Q_TILE
    pending = {g: _nsa_group(g, t0, *refs, seq) for g in range(NSA_KV_HEADS)}
    done = {}
    while pending:
        for g in sorted(pending):
            try:
                next(pending[g])
            except StopIteration as stop:
                done[g] = stop.value
                del pending[g]
    groups = [done[g] for g in sorted(done)]

    def sel_step(c, carry):
        for step, _ in groups:
            step(c)
        return carry

    lax.fori_loop(0, (t0 + SEL_KEY_CHUNK - 1) // SEL_KEY_CHUNK, sel_step, 0)
    for _, finish in groups:
        finish()


def _nsa_attention(qpad, cmp_slabs, kvs, gates, batch, seq):
    qt = NSA_Q_TILE
    dh = NSA_HEAD_DIM
    n_sel = seq // SEL_BLOCK
    n_chunks = seq // CMP_STRIDE
    assert n_sel <= dh and seq % SEL_KEY_CHUNK == 0 and seq >= WINDOW + qt and SEL_KEY_CHUNK % qt == 0
    n_cmp = (seq - CMP_LEN) // CMP_STRIDE + 1
    cmp_start = np.arange(n_chunks) * CMP_STRIDE
    blk = np.arange(LANES) - dh
    overlap_t = ((cmp_start[None, :] < blk[:, None] * SEL_BLOCK + SEL_BLOCK)
                 & (cmp_start[None, :] + CMP_LEN > blk[:, None] * SEL_BLOCK)
                 & (np.arange(n_chunks)[None, :] < n_cmp) & (blk[:, None] >= 0) & (blk[:, None] < n_sel))
    overlap_t = jnp.asarray(overlap_t.astype(np.float32), BF16)
    expand = (np.arange(seq)[None, :] // SEL_BLOCK) == blk[:, None]
    expand = jnp.asarray(expand.astype(np.float32), BF16)
    m_rows = NSA_GROUP * qt
    ng = NSA_KV_HEADS
    return pl.pallas_call(
        functools.partial(_nsa_kernel, seq=seq),
        grid=(batch, seq // qt),
        in_specs=[
            pl.BlockSpec((1, qt, Q_COLS), lambda b, i: (b, i, 0)),
            pl.BlockSpec((1, ng, 2, n_chunks, LANES), lambda b, i: (b, 0, 0, 0, 0)),
            pl.BlockSpec((1, seq, KVS_COLS), lambda b, i: (b, 0, 0)),
            pl.BlockSpec((1, qt, GATE_COLS), lambda b, i: (b, i, 0)),
            pl.BlockSpec((LANES, n_chunks), lambda b, i: (0, 0)),
            pl.BlockSpec((LANES, seq), lambda b, i: (0, 0)),
        ],
        out_specs=pl.BlockSpec((1, qt, NSA_WIDTH), lambda b, i: (b, i, 0)),
        out_shape=jax.ShapeDtypeStruct((batch, seq, NSA_WIDTH), BF16),
        scratch_shapes=[
            pltpu.VMEM((ng, dh, qt), F32),
            pltpu.VMEM((ng, m_rows, LANES), F32),
            pltpu.VMEM((ng, m_rows, LANES), F32),
            pltpu.VMEM((ng, m_rows, LANES), F32),
        ],
        compiler_params=_params(2),
        name="nsa_attention",
    )(qpad, cmp_slabs, kvs, gates, overlap_t, expand)


def _pool_out_kernel(x_ref, nsa_ref, u_ref, uh_ref, pw_ref, ps_ref, wo_ref, g_ref, b_ref, o_ref, *, seq):
    tm = x_ref.shape[0]
    i = pl.program_id(0)
    tiles_per_seq = seq // tm
    first = (i % tiles_per_seq) == 0
    t_seq = (i % tiles_per_seq) * tm + lax.broadcasted_iota(jnp.int32, (tm, 1), 0)
    u = u_ref[...]
    halo = jnp.where(first, 0.0, uh_ref[...])
    z = jnp.concatenate([halo, u], axis=0)
    gd = POOL_GROUP_DIM
    y = jnp.dot(nsa_ref[...], wo_ref[0:NSA_WIDTH, :], preferred_element_type=F32)
    for gi, w in enumerate(POOL_WINDOWS):
        run = z[:, gi * gd:(gi + 1) * gd]
        span = 1
        while span < w:
            run = run + pltpu.roll(run, span, 0)
            span *= 2
        cnt = jnp.minimum(t_seq + 1, w).astype(F32)
        pooled = run[POOL_HALO:, :] / cnt - u[:, gi * gd:(gi + 1) * gd]
        yg = _dot(pooled.astype(BF16), pw_ref[gi]) * ps_ref[:, gi * gd:(gi + 1) * gd]
        y = y + _dot(yg.astype(BF16), wo_ref[NSA_WIDTH + gi * gd:NSA_WIDTH + (gi + 1) * gd, :])
    o_ref[...] = _layer_norm(ALPHA * x_ref[...] + y, g_ref[...], b_ref[...])


def _pool_out(x2d, nsa2d, u2d, pool_w, pool_scale, w_out, ln_g, ln_b, seq):
    t = x2d.shape[0]
    tm = ROW_TILE
    hb = tm // POOL_HALO
    return pl.pallas_call(
        functools.partial(_pool_out_kernel, seq=seq),
        grid=(t // tm,),
        in_specs=[
            pl.BlockSpec((tm, D_MODEL), lambda i: (i, 0)),
            pl.BlockSpec((tm, NSA_WIDTH), lambda i: (i, 0)),
            pl.BlockSpec((tm, POOL_WIDTH), lambda i: (i, 0)),
            pl.BlockSpec((POOL_HALO, POOL_WIDTH), lambda i: (jnp.maximum(i * hb - 1, 0), 0)),
            pl.BlockSpec(pool_w.shape, lambda i: (0, 0, 0)),
            pl.BlockSpec((1, POOL_WIDTH), lambda i: (0, 0)),
            pl.BlockSpec(w_out.shape, lambda i: (0, 0)),
            pl.BlockSpec((1, D_MODEL), lambda i: (0, 0)),
            pl.BlockSpec((1, D_MODEL), lambda i: (0, 0)),
        ],
        out_specs=pl.BlockSpec((tm, D_MODEL), lambda i: (i, 0)),
        out_shape=jax.ShapeDtypeStruct((t, D_MODEL), F32),
        compiler_params=_params(1),
        name="pool_out_ln",
    )(x2d, nsa2d, u2d, u2d, pool_w.astype(BF16), pool_scale.reshape(1, -1), w_out.astype(BF16),
      ln_g.reshape(1, -1), ln_b.reshape(1, -1))


def _sgu_kernel(x_ref, win_ref, ng_ref, nb_ref, ws_ref, bs_ref, wo_ref, g_ref, b_ref, o_ref, sv_scr):
    tm = x_ref.shape[0]
    n_chunk = tm // SG_CHUNK
    gd = SG_GROUP_DIM
    x = x_ref[...]
    xb = x.astype(BF16)
    zu = _gelu(_dot(xb, win_ref[:, 0:SG_WIDTH]))
    zv = _gelu(_dot(xb, win_ref[:, SG_WIDTH:2 * SG_WIDTH]))
    v = _layer_norm(zv, ng_ref[...], nb_ref[...]).astype(BF16)
    row = lax.broadcasted_iota(jnp.int32, (SG_CHUNK, SG_CHUNK), 0)
    col = lax.broadcasted_iota(jnp.int32, (SG_CHUNK, SG_CHUNK), 1)
    causal = col <= row
    for h in range(SG_GROUPS):
        w = jnp.where(causal, ws_ref[h], 0.0).astype(BF16)
        rhs = jnp.concatenate(
            [v[c * SG_CHUNK:(c + 1) * SG_CHUNK, h * gd:(h + 1) * gd] for c in range(n_chunk)], axis=1)
        res = _dot(w, rhs)
        for c in range(n_chunk):
            sv_scr[c * SG_CHUNK:(c + 1) * SG_CHUNK, h * gd:(h + 1) * gd] = res[:, c * gd:(c + 1) * gd]
    bias = bs_ref[...]
    for c in range(n_chunk):
        rows = slice(c * SG_CHUNK, (c + 1) * SG_CHUNK)
        sv_scr[rows, :] = zu[rows, :] * (sv_scr[rows, :] + bias)
    y = _dot(sv_scr[...].astype(BF16), wo_ref[...])
    o_ref[...] = _layer_norm(ALPHA * x + y, g_ref[...], b_ref[...])


def _sgu(x2d, w_in, norm_g, norm_b, w_s, b_s, w_out, ln_g, ln_b):
    t = x2d.shape[0]
    tm = ROW_TILE
    bias = jnp.repeat(b_s.T, SG_GROUP_DIM, axis=1)
    vec = lambda a: a.reshape(1, -1)
    const2 = lambda i: (0, 0)
    return pl.pallas_call(
        _sgu_kernel,
        grid=(t // tm,),
        in_specs=[
            pl.BlockSpec((tm, D_MODEL), lambda i: (i, 0)),
            pl.BlockSpec(w_in.shape, const2),
            pl.BlockSpec((1, SG_WIDTH), const2),
            pl.BlockSpec((1, SG_WIDTH), const2),
            pl.BlockSpec(w_s.shape, lambda i: (0, 0, 0)),
            pl.BlockSpec(bias.shape, const2),
            pl.BlockSpec(w_out.shape, const2),
            pl.BlockSpec((1, D_MODEL), const2),
            pl.BlockSpec((1, D_MODEL), const2),
        ],
        out_specs=pl.BlockSpec((tm, D_MODEL), lambda i: (i, 0)),
        out_shape=jax.ShapeDtypeStruct((t, D_MODEL), F32),
        scratch_shapes=[pltpu.VMEM((tm, SG_WIDTH), F32)],
        compiler_params=_params(1),
        name="sgu_out_ln",
    )(x2d, w_in.astype(BF16), vec(norm_g), vec(norm_b), w_s, bias, w_out.astype(BF16), vec(ln_g), vec(ln_b))


def _mem_kv_kernel(m_ref, w_ref, o_ref):
    o_ref[...] = _dot(m_ref[...].astype(BF16), w_ref[...]).astype(o_ref.dtype)


def _mem_kv(mem2d, wkv):
    rows = mem2d.shape[0]
    n = wkv.shape[1]
    tm = min(ROW_TILE, rows)
    return pl.pallas_call(
        _mem_kv_kernel,
        grid=(rows // tm,),
        in_specs=[pl.BlockSpec((tm, D_MODEL), lambda i: (i, 0)), pl.BlockSpec(wkv.shape, lambda i: (0, 0))],
        out_specs=pl.BlockSpec((tm, n), lambda i: (i, 0)),
        out_shape=jax.ShapeDtypeStruct((rows, n), BF16),
        compiler_params=_params(1),
        name="mem_kv",
    )(mem2d, wkv.astype(BF16))


def _mem_attn_kernel(x_ref, kv_ref, wq_ref, wo_ref, g_ref, b_ref, o_ref):
    x = x_ref[...]
    q = _dot(x.astype(BF16), wq_ref[...]).astype(BF16)
    hd = MEM_HEAD_DIM
    width = MEM_HEADS * hd
    outs = []
    for h in range(MEM_HEADS):
        k = kv_ref[0, :, h * hd:(h + 1) * hd]
        v = kv_ref[0, :, width + h * hd:width + (h + 1) * hd]
        s = _dot_nt(q[:, h * hd:(h + 1) * hd], k)
        e = jnp.exp(s - jnp.max(s, axis=-1, keepdims=True))
        p = e * (1.0 / jnp.sum(e, axis=-1, keepdims=True))
        outs.append(_dot(p.astype(BF16), v).astype(BF16))
    o = jnp.concatenate(outs, axis=1)
    y = _dot(o, wo_ref[...])
    o_ref[...] = _layer_norm(ALPHA * x + y, g_ref[...], b_ref[...])


def _mem_attn(x2d, memkv, wq, wo, ln_g, ln_b, seq):
    t = x2d.shape[0]
    tm = ROW_TILE
    tiles_per_seq = seq // tm
    mem_len, kv_cols = memkv.shape[1], memkv.shape[2]
    const2 = lambda i: (0, 0)
    wq_s = (wq * MEM_HEAD_DIM ** -0.5).astype(BF16)
    return pl.pallas_call(
        _mem_attn_kernel,
        grid=(t // tm,),
        in_specs=[
            pl.BlockSpec((tm, D_MODEL), lambda i: (i, 0)),
            pl.BlockSpec((1, mem_len, kv_cols), lambda i: (i // tiles_per_seq, 0, 0)),
            pl.BlockSpec(wq.shape, const2),
            pl.BlockSpec(wo.shape, const2),
            pl.BlockSpec((1, D_MODEL), const2),
            pl.BlockSpec((1, D_MODEL), const2),
        ],
        out_specs=pl.BlockSpec((tm, D_MODEL), lambda i: (i, 0)),
        out_shape=jax.ShapeDtypeStruct((t, D_MODEL), F32),
        compiler_params=_params(1),
        name="mem_attn_ln",
    )(x2d, memkv, wq_s, wo.astype(BF16), ln_g.reshape(1, -1), ln_b.reshape(1, -1))


def _ffn_kernel(x_ref, xh_ref, wup_ref, cw_ref, cb_ref, wd_ref, g_ref, b_ref, o_ref, *, seq):
    tm = x_ref.shape[0]
    i = pl.program_id(0)
    first = (i % (seq // tm)) == 0
    x = x_ref[...]
    halo = jnp.where(first, 0.0, xh_ref[...])
    xcat = jnp.concatenate([halo, x], axis=0).astype(BF16)
    fc2 = 2 * FFN_CHUNK

    n_steps = D_FF // FFN_CHUNK

    def up(c):
        return _dot(xcat, wup_ref[:, c * fc2:(c + 1) * fc2])

    def shift_rows(v):
        blocks = v.reshape(v.shape[0] // SUBLANES, SUBLANES, v.shape[1])
        rot = pltpu.roll(blocks, 1, 1)
        prev = jnp.concatenate([rot[-1:], rot[:-1]], axis=0)
        sub = lax.broadcasted_iota(jnp.int32, (1, SUBLANES, 1), 1)
        return jnp.where(sub == 0, prev, rot).reshape(v.shape)

    y = None
    acts = []
    h_next = up(0)
    for c in range(n_steps):
        h = h_next
        if c + 1 < n_steps:
            h_next = up(c + 1)
        cw = cw_ref[:, c * fc2:(c + 1) * fc2]
        out = cw[0:1, :] * h
        for j in range(1, CONV_WIDTH):
            out = cw[j:j + 1, :] * h + shift_rows(out)
        out = out[CONV_HALO:, :] + cb_ref[:, c * fc2:(c + 1) * fc2]
        a, gt = out[:, :FFN_CHUNK], out[:, FFN_CHUNK:]
        acts.append((a * (gt * jax.nn.sigmoid(gt))).astype(BF16))
        if len(acts) == FFN_DOWN_GROUP or c + 1 == n_steps:
            r0 = (c + 1 - len(acts)) * FFN_CHUNK
            part = _dot(jnp.concatenate(acts, axis=1), wd_ref[r0:(c + 1) * FFN_CHUNK, :])
            y = part if y is None else y + part
            acts = []
    o_ref[...] = _layer_norm(ALPHA * x + y, g_ref[...], b_ref[...])


def _interleave_ffn_columns(a):
    lead = a.shape[:-1]
    return a.reshape(lead + (2, D_FF // FFN_CHUNK, FFN_CHUNK)).swapaxes(-3, -2).reshape(lead + (2 * D_FF,))


def _ffn(x2d, w_up, conv_w, conv_b, w_down, ln_g, ln_b, seq):
    t = x2d.shape[0]
    tm = FFN_ROW_TILE
    hb = tm // CONV_HALO
    const2 = lambda i: (0, 0)
    w_up, conv_w, conv_b = (_interleave_ffn_columns(a) for a in (w_up, conv_w, conv_b))
    return pl.pallas_call(
        functools.partial(_ffn_kernel, seq=seq),
        grid=(t // tm,),
        in_specs=[
            pl.BlockSpec((tm, D_MODEL), lambda i: (i, 0)),
            pl.BlockSpec((CONV_HALO, D_MODEL), lambda i: (jnp.maximum(i * hb - 1, 0), 0)),
            pl.BlockSpec(w_up.shape, const2, pipeline_mode=pl.Buffered(1)),
            pl.BlockSpec(conv_w.shape, const2),
            pl.BlockSpec((1, 2 * D_FF), const2),
            pl.BlockSpec(w_down.shape, const2, pipeline_mode=pl.Buffered(1)),
            pl.BlockSpec((1, D_MODEL), const2),
            pl.BlockSpec((1, D_MODEL), const2),
        ],
        out_specs=pl.BlockSpec((tm, D_MODEL), lambda i: (i, 0)),
        out_shape=jax.ShapeDtypeStruct((t, D_MODEL), F32),
        compiler_params=_params(1),
        name="conv_ffn_ln",
    )(x2d, x2d, w_up.astype(BF16), conv_w, conv_b.reshape(1, -1), w_down.astype(BF16),
      ln_g.reshape(1, -1), ln_b.reshape(1, -1))


def kernel(x, mem, ab_w_in, nsa_cmp_pe, nsa_cmp_w1, nsa_cmp_w2, pool_w, pool_scale, ab_w_out,
           sg_w_in, sg_norm_g, sg_norm_b, sg_w_s, sg_b_s, sg_w_out, ln_g, ln_b,
           mem_wq, mem_wkv, mem_wo, ffn_w_up, ffn_conv_w, ffn_conv_b, ffn_w_down):
    batch, seq, d = x.shape
    assert d == D_MODEL and seq % ROW_TILE == 0 and seq % FFN_ROW_TILE == 0 and seq % NSA_Q_TILE == 0
    t = batch * seq
    h = x.reshape(t, d)
    mem2d = mem.reshape(batch * mem.shape[1], d)
    for layer in range(DEPTH):
        i = layer // 2
        if layer % 2 == 0:
            qpad, kv0, kvs, gates, u = _in_proj(h, _prep_in_proj_weight(ab_w_in[i]), seq)
            kvc = _cmp_kv(kv0, nsa_cmp_pe[i], nsa_cmp_w1[i], nsa_cmp_w2[i], batch, seq)
            nsa = _nsa_attention(qpad.reshape(batch, seq, Q_COLS), kvc, kvs.reshape(batch, seq, KVS_COLS),
                                 gates.reshape(batch, seq, GATE_COLS), batch, seq)
            h = _pool_out(h, nsa.reshape(t, NSA_WIDTH), u, pool_w[i], pool_scale[i], ab_w_out[i],
                          ln_g[layer, 0], ln_b[layer, 0], seq)
        else:
            h = _sgu(h, sg_w_in[i], sg_norm_g[i], sg_norm_b[i], sg_w_s[i], sg_b_s[i], sg_w_out[i],
                     ln_g[layer, 0], ln_b[layer, 0])
        memkv = _mem_kv(mem2d, mem_wkv[layer]).reshape(batch, mem.shape[1], -1)
        h = _mem_attn(h, memkv, mem_wq[layer], mem_wo[layer], ln_g[layer, 1], ln_b[layer, 1], seq)
        h = _ffn(h, ffn_w_up[layer], ffn_conv_w[layer], ffn_conv_b[layer], ffn_w_down[layer],
                 ln_g[layer, 2], ln_b[layer, 2], seq)
    return h.reshape(batch, seq, d)
```

```python
import functools

import numpy as np
import jax
import jax.numpy as jnp
from jax import lax
from jax.experimental import pallas as pl
from jax.experimental.pallas import tpu as pltpu

F32 = jnp.float32
BF16 = jnp.bfloat16

LANES = 128
SUBLANES = 8
VMEM_LIMIT_BYTES = 56 * 1024 * 1024

D_MODEL = 1024
DEPTH = 2
NSA_HEADS = 8
NSA_KV_HEADS = 2
NSA_GROUP = NSA_HEADS // NSA_KV_HEADS
NSA_HEAD_DIM = 64
NSA_WIDTH = NSA_HEADS * NSA_HEAD_DIM
N_BRANCH = 3
CMP_LEN = 32
CMP_STRIDE = 16
CMP_HIDDEN = 2 * NSA_HEAD_DIM
SEL_BLOCK = 64
SEL_TOP = 16
WINDOW = 512
FORCE_BONUS = 100.0
KV_WIDTH = N_BRANCH * 2 * NSA_KV_HEADS * NSA_HEAD_DIM
GATE_WIDTH = NSA_HEADS * N_BRANCH
POOL_WINDOWS = (2, 4, 8, 16)
POOL_WIDTH = D_MODEL // 2
POOL_GROUP_DIM = POOL_WIDTH // len(POOL_WINDOWS)
SG_CHUNK = 128
SG_GROUPS = 8
SG_WIDTH = D_MODEL
SG_GROUP_DIM = SG_WIDTH // SG_GROUPS
MEM_HEADS = 4
MEM_HEAD_DIM = D_MODEL // MEM_HEADS
D_FF = 256 * ((8 * D_MODEL // 3 + 255) // 256)
CONV_WIDTH = 3
ALPHA = (2 * DEPTH) ** 0.25
LN_EPS = 1e-5
NEG_INF = -1e30

ROW_TILE = 512
FFN_ROW_TILE = 1024
NSA_Q_TILE = 256
SEL_KEY_CHUNK = 512
FFN_CHUNK = 256
FFN_DOWN_GROUP = 4
POOL_HALO = 16
CONV_HALO = SUBLANES


def _params(n_parallel):
    return pltpu.CompilerParams(
        dimension_semantics=("parallel",) * n_parallel,
        vmem_limit_bytes=VMEM_LIMIT_BYTES,
    )


def _layer_norm(r, g, b):
    mu = jnp.mean(r, axis=-1, keepdims=True)
    d = r - mu
    var = jnp.mean(d * d, axis=-1, keepdims=True)
    return d * lax.rsqrt(var + LN_EPS) * g + b


def _gelu(x):
    c = np.sqrt(2.0 / np.pi).astype(np.float32)
    return 0.5 * x * (1.0 + jnp.tanh(c * (x + 0.044715 * (x * x * x))))


def _dot(a, b):
    return jnp.dot(a, b, preferred_element_type=F32)


def _dot_nt(a, b):
    return lax.dot_general(a, b, (((1,), (1,)), ((), ())), preferred_element_type=F32)


LOG2E = float(np.log2(np.e))
Q_COLS = NSA_HEADS * LANES
KV0_COLS = NSA_KV_HEADS * LANES
SLAB_KSEL, SLAB_VSEL, SLAB_KWIN, SLAB_VWIN = range(4)
N_SLAB = 4
KVS_COLS = N_SLAB * NSA_KV_HEADS * LANES
GATE_COLS = LANES
KV12_COLS = 2 * NSA_KV_HEADS * LANES
IN_PROJ_SPLITS = [int(v) for v in np.cumsum([0, NSA_WIDTH, KV0_COLS, KV12_COLS, GATE_COLS, POOL_WIDTH])]


def _in_proj_kernel(x_ref, w_ref, q_ref, kv0_ref, kvs_ref, gate_ref, u_ref, *, seq):
    tm = x_ref.shape[0]
    ng = NSA_KV_HEADS
    xb = x_ref[...].astype(BF16)

    def cols(k):
        return _dot(xb, w_ref[:, IN_PROJ_SPLITS[k]:IN_PROJ_SPLITS[k + 1]])

    def tile(a, k):
        return a[:, k * LANES:(k + 1) * LANES]

    def put(ref, k, lower_part, fill):
        ref[:, k * LANES:(k + 1) * LANES] = jnp.where(lower, lower_part, fill).astype(ref.dtype)

    pos = (pl.program_id(0) % (seq // tm)) * tm + lax.broadcasted_iota(jnp.int32, (tm, 1), 0)
    lane = lax.broadcasted_iota(jnp.int32, (1, LANES), 1)
    lower = lane < NSA_HEAD_DIM
    onehot = jnp.where(pos // SEL_BLOCK == lane - NSA_HEAD_DIM, 1.0, 0.0)

    q = cols(0) * LOG2E
    for pair in range(NSA_HEADS // 2):
        both = tile(q, pair)
        put(q_ref, 2 * pair, both, 0.0)
        put(q_ref, 2 * pair + 1, pltpu.roll(both, NSA_HEAD_DIM, 1), 0.0)
    kv0_ref[...] = cols(1)
    gate_ref[...] = cols(3)
    u_ref[...] = cols(4)
    kv = cols(2)
    for branch, (kslab, vslab, kfill) in enumerate(((SLAB_KSEL, SLAB_VSEL, onehot), (SLAB_KWIN, SLAB_VWIN, 0.0))):
        for g in range(ng):
            both = tile(kv, branch * ng + g)
            put(kvs_ref, kslab * ng + g, both, kfill)
            put(kvs_ref, vslab * ng + g, pltpu.roll(both, NSA_HEAD_DIM, 1), 1.0)


def _in_proj(x2d, w, seq):
    t = x2d.shape[0]
    n = w.shape[1]
    widths = [Q_COLS, KV0_COLS, KVS_COLS, GATE_COLS, POOL_WIDTH]
    dtypes = [BF16, F32, BF16, F32, F32]
    return pl.pallas_call(
        functools.partial(_in_proj_kernel, seq=seq),
        grid=(t // ROW_TILE,),
        in_specs=[
            pl.BlockSpec((ROW_TILE, D_MODEL), lambda i: (i, 0)),
            pl.BlockSpec((D_MODEL, n), lambda i: (0, 0)),
        ],
        out_specs=[pl.BlockSpec((ROW_TILE, wd), lambda i: (i, 0)) for wd in widths],
        out_shape=[jax.ShapeDtypeStruct((t, wd), dt) for wd, dt in zip(widths, dtypes)],
        compiler_params=_params(1),
        name="in_proj",
    )(x2d, w)


def _prep_in_proj_weight(w):
    d = w.shape[0]
    o_q = NSA_WIDTH
    o_kv = o_q + KV_WIDTH
    o_g = o_kv + GATE_WIDTH
    wq = w[:, :o_q] * NSA_HEAD_DIM ** -0.5
    wkv = w[:, o_q:o_kv].reshape(d, N_BRANCH, 2, NSA_KV_HEADS, NSA_HEAD_DIM)
    wkv = wkv.transpose(0, 1, 3, 2, 4)
    wkv0 = wkv[:, 0].reshape(d, KV0_COLS)
    wkv12 = wkv[:, 1:].reshape(d, KV12_COLS)
    wg = jnp.pad(w[:, o_kv:o_g], ((0, 0), (0, GATE_COLS - GATE_WIDTH)))
    wu = w[:, o_g:]
    return jnp.concatenate([wq, wkv0, wkv12, wg, wu], axis=1).astype(BF16)


def _cmp_kv_kernel(x_ref, pe_ref, w1_ref, w2_ref, o_ref):
    n_chunks = x_ref.shape[1] // CMP_STRIDE
    lane = lax.broadcasted_iota(jnp.int32, (1, LANES), 1)
    half = [None, None]
    for l in range(CMP_STRIDE):
        rows = x_ref[0, pl.ds(l, n_chunks, stride=CMP_STRIDE), :]
        for h in range(2):
            part = _dot((rows + pe_ref[h, l:l + 1, :]).astype(BF16), w1_ref[h, l])
            half[h] = part if half[h] is None else half[h] + part
    hid = _gelu(half[0] + pltpu.roll(half[1], n_chunks - 1, 0))
    for kv in range(2):
        part = _dot(hid[:, kv * CMP_HIDDEN:(kv + 1) * CMP_HIDDEN].astype(BF16), w2_ref[kv])
        if kv == 1:
            part = jnp.where(lane >= NSA_HEAD_DIM, 1.0, part)
        o_ref[0, 0, kv] = part.astype(o_ref.dtype)


def _cmp_kv(kv0, pe, w1, w2, batch, seq):
    n_chunks = seq // CMP_STRIDE
    dh = NSA_HEAD_DIM
    x = kv0.reshape(batch, seq, KV0_COLS)
    pe_t = pe.reshape(2, 2, CMP_STRIDE, dh).transpose(1, 2, 0, 3).reshape(2, CMP_STRIDE, LANES)
    w1r = w1.reshape(2, 2, CMP_STRIDE, dh, CMP_HIDDEN)
    zero = jnp.zeros_like(w1r[0])
    w1_t = jnp.concatenate([jnp.concatenate([w1r[0], zero], axis=-1),
                            jnp.concatenate([zero, w1r[1]], axis=-1)], axis=-2).astype(BF16)
    w2p = jnp.pad(w2, ((0, 0), (0, 0), (0, LANES - dh))).astype(BF16)
    return pl.pallas_call(
        _cmp_kv_kernel,
        grid=(batch, NSA_KV_HEADS),
        in_specs=[
            pl.BlockSpec((1, seq, LANES), lambda b, g: (b, 0, g)),
            pl.BlockSpec((2, CMP_STRIDE, LANES), lambda b, g: (0, 0, 0)),
            pl.BlockSpec((2, CMP_STRIDE, LANES, 2 * CMP_HIDDEN), lambda b, g: (0, 0, 0, 0)),
            pl.BlockSpec((2, CMP_HIDDEN, LANES), lambda b, g: (0, 0, 0)),
        ],
        out_specs=pl.BlockSpec((1, 1, 2, n_chunks, LANES), lambda b, g: (b, g, 0, 0, 0)),
        out_shape=jax.ShapeDtypeStruct((batch, NSA_KV_HEADS, 2, n_chunks, LANES), BF16),
        compiler_params=_params(2),
        name="cmp_kv",
    )(x, pe_t, w1_t, w2p)


def _split3(x):
    a = x.astype(BF16)
    r = x - a.astype(F32)
    b = r.astype(BF16)
    c = (r - b.astype(F32)).astype(BF16)
    return a, b, c


def _lane_tiles(s):
    return [s[:, i * LANES:(i + 1) * LANES] for i in range(s.shape[1] // LANES)]


def _row_max_bcast(tiles):
    mt = tiles[0]
    for t in tiles[1:]:
        mt = jnp.maximum(mt, t)
    return jnp.broadcast_to(jnp.max(mt, axis=-1, keepdims=True), mt.shape)


def _nsa_group(g, t0, q_ref, cmp_ref, kvs_ref, gate_ref, ovl_ref, exp_ref, o_ref,
               sc_scr, m_scr, acc_scr, out_scr, seq):
    qt = NSA_Q_TILE
    rr = NSA_GROUP
    dh = NSA_HEAD_DIM
    m_rows = rr * qt
    n_cmp_rows = cmp_ref.shape[3]
    n_sel = seq // SEL_BLOCK
    win_keys = WINDOW + qt

    def slab(k):
        c0 = (k * NSA_KV_HEADS + g) * LANES
        return kvs_ref.at[:, :, c0:c0 + LANES]

    ksel_ref, vsel_ref, kwin_ref, vwin_ref = (slab(k) for k in (SLAB_KSEL, SLAB_VSEL, SLAB_KWIN, SLAB_VWIN))
    q_ref = q_ref.at[:, :, g * rr * LANES:(g + 1) * rr * LANES]
    cmp_ref = cmp_ref.at[:, g:g + 1]
    o_ref = o_ref.at[:, :, g * rr * dh:(g + 1) * rr * dh]
    sc_scr, m_scr, acc_scr, out_scr = (r.at[g] for r in (sc_scr, m_scr, acc_scr, out_scr))

    qb = q_ref[0]
    qp = jnp.concatenate([qb[:, r * LANES:(r + 1) * LANES] for r in range(rr)], axis=0)
    tq = t0 + lax.broadcasted_iota(jnp.int32, (qt, 1), 0)
    lane = lax.broadcasted_iota(jnp.int32, (1, LANES), 1)
    lower = lane < dh
    gate = jax.nn.sigmoid(gate_ref[0])

    def emit(branch, o_norm, first):
        for r in range(rr):
            c = (g * rr + r) * N_BRANCH + branch
            contrib = gate[:, c:c + 1] * o_norm[r * qt:(r + 1) * qt]
            if first:
                out_scr[r * qt:(r + 1) * qt, :] = contrib
            else:
                out_scr[r * qt:(r + 1) * qt, :] += contrib

    def normalize(acc):
        inv = jnp.where(lower, 1.0 / pltpu.roll(acc, dh, 1), 0.0)
        return acc * inv

    kcmp = cmp_ref[0, 0, 0]
    vcmp = cmp_ref[0, 0, 1]
    s = _dot_nt(qp, kcmp).reshape(rr, qt, n_cmp_rows)
    n_idx = lax.broadcasted_iota(jnp.int32, (1, n_cmp_rows), 1)
    n_cmp = (seq - CMP_LEN) // CMP_STRIDE + 1
    valid = (n_idx * CMP_STRIDE + (CMP_LEN - 1) <= tq) & (n_idx < n_cmp)
    s = s + jnp.where(valid, 0.0, NEG_INF)[None]
    e = jnp.exp2(s - jnp.max(s, axis=-1, keepdims=True))
    inv = 1.0 / jnp.sum(e, axis=-1, keepdims=True)
    has_cmp = (tq >= CMP_LEN - 1).astype(F32)
    p = e * (inv * has_cmp[None])
    emit(0, _dot(p.reshape(m_rows, n_cmp_rows).astype(BF16), vcmp), True)
    yield

    w0 = pl.multiple_of(jnp.maximum(t0 - WINDOW, 0), qt)
    kw = kwin_ref[0, pl.ds(w0, win_keys), :]
    vw = vwin_ref[0, pl.ds(w0, win_keys), :]
    wpos = w0 + lax.broadcasted_iota(jnp.int32, (1, win_keys), 1)
    okw = (wpos <= tq) & (wpos > tq - WINDOW)
    s = _dot_nt(qp, kw).reshape(rr, qt, win_keys) + jnp.where(okw, 0.0, NEG_INF)[None]
    tiles = _lane_tiles(s.reshape(m_rows, win_keys))
    m_b = _row_max_bcast(tiles)
    pw = jnp.concatenate([jnp.exp2(t - m_b) for t in tiles], axis=1).astype(BF16)
    emit(2, normalize(_dot(pw, vw)), False)
    yield

    psum = p[0]
    for r in range(1, rr):
        psum = psum + p[r]
    ovl_t = ovl_ref[...]
    imp_t = None
    for term in _split3(psum):
        part = _dot_nt(ovl_t, term)
        imp_t = part if imp_t is None else imp_t + part
    imp = imp_t[dh:, :]
    j_s = lax.broadcasted_iota(jnp.int32, (dh, 1), 0)
    tq_l = t0 + lax.broadcasted_iota(jnp.int32, (1, qt), 1)
    cur = tq_l // SEL_BLOCK
    blk_ok = j_s * SEL_BLOCK <= tq_l
    forced = (j_s == 0) | (j_s == cur) | (j_s == cur - 1)
    score = jnp.where(blk_ok, imp + jnp.where(forced, FORCE_BONUS, 0.0), NEG_INF)
    score = jnp.where(j_s < n_sel, score, -jnp.inf)
    sc_scr[...] = score
    yield
    n_grp = dh // SUBLANES
    grp = [score[k * SUBLANES:(k + 1) * SUBLANES] for k in range(n_grp)]
    rank = [jnp.zeros((SUBLANES, qt), F32) for _ in range(n_grp)]
    sub = lax.broadcasted_iota(jnp.int32, (SUBLANES, 1), 0)
    for jp in range(dh):
        row = jnp.broadcast_to(sc_scr[jp:jp + 1, :], (SUBLANES, qt))
        kj = jp // SUBLANES
        for k in range(n_grp):
            if k > kj:
                rank[k] = rank[k] + jnp.where(row >= grp[k], 1.0, 0.0)
            elif k < kj:
                rank[k] = rank[k] + jnp.where(row > grp[k], 1.0, 0.0)
            else:
                ge = jnp.where(row >= grp[k], 1.0, 0.0)
                gt = jnp.where(row > grp[k], 1.0, 0.0)
                rank[k] = rank[k] + jnp.where(sub > jp % SUBLANES, ge, gt)
    n_top = float(min(SEL_TOP, n_sel))
    sel_t = jnp.concatenate(rank, axis=0) < n_top
    sel_t = jnp.where(sel_t & blk_ok, 1.0, 0.0)
    sel = jnp.concatenate([jnp.zeros((dh, qt), F32), sel_t], axis=0).T
    yield

    d0 = pl.multiple_of(t0, qt)
    chosen = _dot(sel.astype(BF16), exp_ref[:, pl.ds(d0, qt)])
    kpos = t0 + lax.broadcasted_iota(jnp.int32, (1, qt), 1)
    okd = (chosen > 0.5) & (kpos <= tq)
    kd = ksel_ref[0, pl.ds(d0, qt), :]
    vd = vsel_ref[0, pl.ds(d0, qt), :]
    s = (_dot_nt(qp, kd).reshape(rr, qt, qt) + jnp.where(okd, 0.0, NEG_INF)[None]).reshape(m_rows, qt)
    m_b = _row_max_bcast(_lane_tiles(s))
    m_scr[...] = m_b
    acc_scr[...] = _dot(jnp.concatenate([jnp.exp2(t - m_b) for t in _lane_tiles(s)], axis=1).astype(BF16), vd)
    blk_lane = lane - dh
    bias = jnp.where((sel > 0.5) & (blk_lane * SEL_BLOCK < t0), 0.0, NEG_INF).astype(BF16)
    q_aug = jnp.concatenate([jnp.where(lower, qp[r * qt:(r + 1) * qt], bias) for r in range(rr)], axis=0)
    kc = SEL_KEY_CHUNK

    def step(c):
        k0 = pl.multiple_of(c * kc, kc)
        ks = ksel_ref[0, pl.ds(k0, kc), :]
        vs = vsel_ref[0, pl.ds(k0, kc), :]
        tiles = _lane_tiles(_dot_nt(q_aug, ks))
        m_old = m_scr[...]
        m_new = jnp.maximum(m_old, _row_max_bcast(tiles))
        alpha = jnp.exp2(m_old - m_new)
        pexp = jnp.concatenate([jnp.exp2(t - m_new) for t in tiles], axis=1).astype(BF16)
        acc_scr[...] = alpha * acc_scr[...] + _dot(pexp, vs)
        m_scr[...] = m_new

    def finish():
        emit(1, normalize(acc_scr[...]), False)
        for pair in range(rr // 2):
            even = out_scr[(2 * pair) * qt:(2 * pair + 1) * qt, :]
            odd = pltpu.roll(out_scr[(2 * pair + 1) * qt:(2 * pair + 2) * qt, :], dh, 1)
            o_ref[0, :, pair * LANES:(pair + 1) * LANES] = jnp.where(lower, even, odd).astype(o_ref.dtype)

    return step, finish


def _nsa_kernel(*refs, seq):
    t0 = pl.program_id(1) * NSA_Q_TILE
    pending = {g: _nsa_group(g, t0, *refs, seq) for g in range(NSA_KV_HEADS)}
    done = {}
    while pending:
        for g in sorted(pending):
            try:
                next(pending[g])
            except StopIteration as stop:
                done[g] = stop.value
                del pending[g]
    groups = [done[g] for g in sorted(done)]

    def sel_step(c, carry):
        for step, _ in groups:
            step(c)
        return carry

    lax.fori_loop(0, (t0 + SEL_KEY_CHUNK - 1) // SEL_KEY_CHUNK, sel_step, 0)
    for _, finish in groups:
        finish()


def _nsa_attention(qpad, cmp_slabs, kvs, gates, batch, seq):
    qt = NSA_Q_TILE
    dh = NSA_HEAD_DIM
    n_sel = seq // SEL_BLOCK
    n_chunks = seq // CMP_STRIDE
    assert n_sel <= dh and seq % SEL_KEY_CHUNK == 0 and seq >= WINDOW + qt and SEL_KEY_CHUNK % qt == 0
    n_cmp = (seq - CMP_LEN) // CMP_STRIDE + 1
    cmp_start = np.arange(n_chunks) * CMP_STRIDE
    blk = np.arange(LANES) - dh
    overlap_t = ((cmp_start[None, :] < blk[:, None] * SEL_BLOCK + SEL_BLOCK)
                 & (cmp_start[None, :] + CMP_LEN > blk[:, None] * SEL_BLOCK)
                 & (np.arange(n_chunks)[None, :] < n_cmp) & (blk[:, None] >= 0) & (blk[:, None] < n_sel))
    overlap_t = jnp.asarray(overlap_t.astype(np.float32), BF16)
    expand = (np.arange(seq)[None, :] // SEL_BLOCK) == blk[:, None]
    expand = jnp.asarray(expand.astype(np.float32), BF16)
    m_rows = NSA_GROUP * qt
    ng = NSA_KV_HEADS
    return pl.pallas_call(
        functools.partial(_nsa_kernel, seq=seq),
        grid=(batch, seq // qt),
        in_specs=[
            pl.BlockSpec((1, qt, Q_COLS), lambda b, i: (b, i, 0)),
            pl.BlockSpec((1, ng, 2, n_chunks, LANES), lambda b, i: (b, 0, 0, 0, 0)),
            pl.BlockSpec((1, seq, KVS_COLS), lambda b, i: (b, 0, 0)),
            pl.BlockSpec((1, qt, GATE_COLS), lambda b, i: (b, i, 0)),
            pl.BlockSpec((LANES, n_chunks), lambda b, i: (0, 0)),
            pl.BlockSpec((LANES, seq), lambda b, i: (0, 0)),
        ],
        out_specs=pl.BlockSpec((1, qt, NSA_WIDTH), lambda b, i: (b, i, 0)),
        out_shape=jax.ShapeDtypeStruct((batch, seq, NSA_WIDTH), BF16),
        scratch_shapes=[
            pltpu.VMEM((ng, dh, qt), F32),
            pltpu.VMEM((ng, m_rows, LANES), F32),
            pltpu.VMEM((ng, m_rows, LANES), F32),
            pltpu.VMEM((ng, m_rows, LANES), F32),
        ],
        compiler_params=_params(2),
        name="nsa_attention",
    )(qpad, cmp_slabs, kvs, gates, overlap_t, expand)


def _pool_out_kernel(x_ref, nsa_ref, u_ref, uh_ref, pw_ref, ps_ref, wo_ref, g_ref, b_ref, o_ref, *, seq):
    tm = x_ref.shape[0]
    i = pl.program_id(0)
    tiles_per_seq = seq // tm
    first = (i % tiles_per_seq) == 0
    t_seq = (i % tiles_per_seq) * tm + lax.broadcasted_iota(jnp.int32, (tm, 1), 0)
    u = u_ref[...]
    halo = jnp.where(first, 0.0, uh_ref[...])
    z = jnp.concatenate([halo, u], axis=0)
    gd = POOL_GROUP_DIM
    y = jnp.dot(nsa_ref[...], wo_ref[0:NSA_WIDTH, :], preferred_element_type=F32)
    for gi, w in enumerate(POOL_WINDOWS):
        run = z[:, gi * gd:(gi + 1) * gd]
        span = 1
        while span < w:
            run = run + pltpu.roll(run, span, 0)
            span *= 2
        cnt = jnp.minimum(t_seq + 1, w).astype(F32)
        pooled = run[POOL_HALO:, :] / cnt - u[:, gi * gd:(gi + 1) * gd]
        yg = _dot(pooled.astype(BF16), pw_ref[gi]) * ps_ref[:, gi * gd:(gi + 1) * gd]
        y = y + _dot(yg.astype(BF16), wo_ref[NSA_WIDTH + gi * gd:NSA_WIDTH + (gi + 1) * gd, :])
    o_ref[...] = _layer_norm(ALPHA * x_ref[...] + y, g_ref[...], b_ref[...])


def _pool_out(x2d, nsa2d, u2d, pool_w, pool_scale, w_out, ln_g, ln_b, seq):
    t = x2d.shape[0]
    tm = ROW_TILE
    hb = tm // POOL_HALO
    return pl.pallas_call(
        functools.partial(_pool_out_kernel, seq=seq),
        grid=(t // tm,),
        in_specs=[
            pl.BlockSpec((tm, D_MODEL), lambda i: (i, 0)),
            pl.BlockSpec((tm, NSA_WIDTH), lambda i: (i, 0)),
            pl.BlockSpec((tm, POOL_WIDTH), lambda i: (i, 0)),
            pl.BlockSpec((POOL_HALO, POOL_WIDTH), lambda i: (jnp.maximum(i * hb - 1, 0), 0)),
            pl.BlockSpec(pool_w.shape, lambda i: (0, 0, 0)),
            pl.BlockSpec((1, POOL_WIDTH), lambda i: (0, 0)),
            pl.BlockSpec(w_out.shape, lambda i: (0, 0)),
            pl.BlockSpec((1, D_MODEL), lambda i: (0, 0)),
            pl.BlockSpec((1, D_MODEL), lambda i: (0, 0)),
        ],
        out_specs=pl.BlockSpec((tm, D_MODEL), lambda i: (i, 0)),
        out_shape=jax.ShapeDtypeStruct((t, D_MODEL), F32),
        compiler_params=_params(1),
        name="pool_out_ln",
    )(x2d, nsa2d, u2d, u2d, pool_w.astype(BF16), pool_scale.reshape(1, -1), w_out.astype(BF16),
      ln_g.reshape(1, -1), ln_b.reshape(1, -1))


def _sgu_kernel(x_ref, win_ref, ng_ref, nb_ref, ws_ref, bs_ref, wo_ref, g_ref, b_ref, o_ref, sv_scr):
    tm = x_ref.shape[0]
    n_chunk = tm // SG_CHUNK
    gd = SG_GROUP_DIM
    x = x_ref[...]
    xb = x.astype(BF16)
    zu = _gelu(_dot(xb, win_ref[:, 0:SG_WIDTH]))
    zv = _gelu(_dot(xb, win_ref[:, SG_WIDTH:2 * SG_WIDTH]))
    v = _layer_norm(zv, ng_ref[...], nb_ref[...]).astype(BF16)
    row = lax.broadcasted_iota(jnp.int32, (SG_CHUNK, SG_CHUNK), 0)
    col = lax.broadcasted_iota(jnp.int32, (SG_CHUNK, SG_CHUNK), 1)
    causal = col <= row
    for h in range(SG_GROUPS):
        w = jnp.where(causal, ws_ref[h], 0.0).astype(BF16)
        rhs = jnp.concatenate(
            [v[c * SG_CHUNK:(c + 1) * SG_CHUNK, h * gd:(h + 1) * gd] for c in range(n_chunk)], axis=1)
        res = _dot(w, rhs)
        for c in range(n_chunk):
            sv_scr[c * SG_CHUNK:(c + 1) * SG_CHUNK, h * gd:(h + 1) * gd] = res[:, c * gd:(c + 1) * gd]
    bias = bs_ref[...]
    for c in range(n_chunk):
        rows = slice(c * SG_CHUNK, (c + 1) * SG_CHUNK)
        sv_scr[rows, :] = zu[rows, :] * (sv_scr[rows, :] + bias)
    y = _dot(sv_scr[...].astype(BF16), wo_ref[...])
    o_ref[...] = _layer_norm(ALPHA * x + y, g_ref[...], b_ref[...])


def _sgu(x2d, w_in, norm_g, norm_b, w_s, b_s, w_out, ln_g, ln_b):
    t = x2d.shape[0]
    tm = ROW_TILE
    bias = jnp.repeat(b_s.T, SG_GROUP_DIM, axis=1)
    vec = lambda a: a.reshape(1, -1)
    const2 = lambda i: (0, 0)
    return pl.pallas_call(
        _sgu_kernel,
        grid=(t // tm,),
        in_specs=[
            pl.BlockSpec((tm, D_MODEL), lambda i: (i, 0)),
            pl.BlockSpec(w_in.shape, const2),
            pl.BlockSpec((1, SG_WIDTH), const2),
            pl.BlockSpec((1, SG_WIDTH), const2),
            pl.BlockSpec(w_s.shape, lambda i: (0, 0, 0)),
            pl.BlockSpec(bias.shape, const2),
            pl.BlockSpec(w_out.shape, const2),
            pl.BlockSpec((1, D_MODEL), const2),
            pl.BlockSpec((1, D_MODEL), const2),
        ],
        out_specs=pl.BlockSpec((tm, D_MODEL), lambda i: (i, 0)),
        out_shape=jax.ShapeDtypeStruct((t, D_MODEL), F32),
        scratch_shapes=[pltpu.VMEM((tm, SG_WIDTH), F32)],
        compiler_params=_params(1),
        name="sgu_out_ln",
    )(x2d, w_in.astype(BF16), vec(norm_g), vec(norm_b), w_s, bias, w_out.astype(BF16), vec(ln_g), vec(ln_b))


def _mem_kv_kernel(m_ref, w_ref, o_ref):
    o_ref[...] = _dot(m_ref[...].astype(BF16), w_ref[...]).astype(o_ref.dtype)


def _mem_kv(mem2d, wkv):
    rows = mem2d.shape[0]
    n = wkv.shape[1]
    tm = min(ROW_TILE, rows)
    return pl.pallas_call(
        _mem_kv_kernel,
        grid=(rows // tm,),
        in_specs=[pl.BlockSpec((tm, D_MODEL), lambda i: (i, 0)), pl.BlockSpec(wkv.shape, lambda i: (0, 0))],
        out_specs=pl.BlockSpec((tm, n), lambda i: (i, 0)),
        out_shape=jax.ShapeDtypeStruct((rows, n), BF16),
        compiler_params=_params(1),
        name="mem_kv",
    )(mem2d, wkv.astype(BF16))


def _mem_attn_kernel(x_ref, kv_ref, wq_ref, wo_ref, g_ref, b_ref, o_ref):
    x = x_ref[...]
    q = _dot(x.astype(BF16), wq_ref[...]).astype(BF16)
    hd = MEM_HEAD_DIM
    width = MEM_HEADS * hd
    outs = []
    for h in range(MEM_HEADS):
        k = kv_ref[0, :, h * hd:(h + 1) * hd]
        v = kv_ref[0, :, width + h * hd:width + (h + 1) * hd]
        s = _dot_nt(q[:, h * hd:(h + 1) * hd], k)
        e = jnp.exp(s - jnp.max(s, axis=-1, keepdims=True))
        p = e * (1.0 / jnp.sum(e, axis=-1, keepdims=True))
        outs.append(_dot(p.astype(BF16), v).astype(BF16))
    o = jnp.concatenate(outs, axis=1)
    y = _dot(o, wo_ref[...])
    o_ref[...] = _layer_norm(ALPHA * x + y, g_ref[...], b_ref[...])


def _mem_attn(x2d, memkv, wq, wo, ln_g, ln_b, seq):
    t = x2d.shape[0]
    tm = ROW_TILE
    tiles_per_seq = seq // tm
    mem_len, kv_cols = memkv.shape[1], memkv.shape[2]
    const2 = lambda i: (0, 0)
    wq_s = (wq * MEM_HEAD_DIM ** -0.5).astype(BF16)
    return pl.pallas_call(
        _mem_attn_kernel,
        grid=(t // tm,),
        in_specs=[
            pl.BlockSpec((tm, D_MODEL), lambda i: (i, 0)),
            pl.BlockSpec((1, mem_len, kv_cols), lambda i: (i // tiles_per_seq, 0, 0)),
            pl.BlockSpec(wq.shape, const2),
            pl.BlockSpec(wo.shape, const2),
            pl.BlockSpec((1, D_MODEL), const2),
            pl.BlockSpec((1, D_MODEL), const2),
        ],
        out_specs=pl.BlockSpec((tm, D_MODEL), lambda i: (i, 0)),
        out_shape=jax.ShapeDtypeStruct((t, D_MODEL), F32),
        compiler_params=_params(1),
        name="mem_attn_ln",
    )(x2d, memkv, wq_s, wo.astype(BF16), ln_g.reshape(1, -1), ln_b.reshape(1, -1))


def _ffn_kernel(x_ref, xh_ref, wup_ref, cw_ref, cb_ref, wd_ref, g_ref, b_ref, o_ref, *, seq):
    tm = x_ref.shape[0]
    i = pl.program_id(0)
    first = (i % (seq // tm)) == 0
    x = x_ref[...]
    halo = jnp.where(first, 0.0, xh_ref[...])
    xcat = jnp.concatenate([halo, x], axis=0).astype(BF16)
    n_steps = D_FF // FFN_CHUNK

    def cols(ref, c):
        lo = c * FFN_CHUNK
        return jnp.concatenate([ref[:, lo:lo + FFN_CHUNK], ref[:, D_FF + lo:D_FF + lo + FFN_CHUNK]], axis=1)

    def up(c):
        lo = c * FFN_CHUNK
        return jnp.concatenate([_dot(xcat, wup_ref[:, lo:lo + FFN_CHUNK]),
                                _dot(xcat, wup_ref[:, D_FF + lo:D_FF + lo + FFN_CHUNK])], axis=1)

    def shift_rows(v):
        blocks = v.reshape(v.shape[0] // SUBLANES, SUBLANES, v.shape[1])
        rot = pltpu.roll(blocks, 1, 1)
        prev = jnp.concatenate([rot[-1:], rot[:-1]], axis=0)
        sub = lax.broadcasted_iota(jnp.int32, (1, SUBLANES, 1), 1)
        return jnp.where(sub == 0, prev, rot).reshape(v.shape)

    y = None
    acts = []
    h_next = up(0)
    for c in range(n_steps):
        h = h_next
        if c + 1 < n_steps:
            h_next = up(c + 1)
        cw = cols(cw_ref, c)
        out = cw[0:1, :] * h
        for j in range(1, CONV_WIDTH):
            out = cw[j:j + 1, :] * h + shift_rows(out)
        out = out[CONV_HALO:, :] + cols(cb_ref, c)
        a, gt = out[:, :FFN_CHUNK], out[:, FFN_CHUNK:]
        acts.append((a * (gt * jax.nn.sigmoid(gt))).astype(BF16))
        if len(acts) == FFN_DOWN_GROUP or c + 1 == n_steps:
            r0 = (c + 1 - len(acts)) * FFN_CHUNK
            part = _dot(jnp.concatenate(acts, axis=1), wd_ref[r0:(c + 1) * FFN_CHUNK, :])
            y = part if y is None else y + part
            acts = []
    o_ref[...] = _layer_norm(ALPHA * x + y, g_ref[...], b_ref[...])


def _ffn(x2d, w_up, conv_w, conv_b, w_down, ln_g, ln_b, seq):
    t = x2d.shape[0]
    tm = FFN_ROW_TILE
    hb = tm // CONV_HALO
    const2 = lambda i: (0, 0)
    return pl.pallas_call(
        functools.partial(_ffn_kernel, seq=seq),
        grid=(t // tm,),
        in_specs=[
            pl.BlockSpec((tm, D_MODEL), lambda i: (i, 0)),
            pl.BlockSpec((CONV_HALO, D_MODEL), lambda i: (jnp.maximum(i * hb - 1, 0), 0)),
            pl.BlockSpec(w_up.shape, const2, pipeline_mode=pl.Buffered(1)),
            pl.BlockSpec(conv_w.shape, const2),
            pl.BlockSpec((1, 2 * D_FF), const2),
            pl.BlockSpec(w_down.shape, const2, pipeline_mode=pl.Buffered(1)),
            pl.BlockSpec((1, D_MODEL), const2),
            pl.BlockSpec((1, D_MODEL), const2),
        ],
        out_specs=pl.BlockSpec((tm, D_MODEL), lambda i: (i, 0)),
        out_shape=jax.ShapeDtypeStruct((t, D_MODEL), F32),
        compiler_params=_params(1),
        name="conv_ffn_ln",
    )(x2d, x2d, w_up.astype(BF16), conv_w, conv_b.reshape(1, -1), w_down.astype(BF16),
      ln_g.reshape(1, -1), ln_b.reshape(1, -1))


def kernel(x, mem, ab_w_in, nsa_cmp_pe, nsa_cmp_w1, nsa_cmp_w2, pool_w, pool_scale, ab_w_out,
           sg_w_in, sg_norm_g, sg_norm_b, sg_w_s, sg_b_s, sg_w_out, ln_g, ln_b,
           mem_wq, mem_wkv, mem_wo, ffn_w_up, ffn_conv_w, ffn_conv_b, ffn_w_down):
    batch, seq, d = x.shape
    assert d == D_MODEL and seq % ROW_TILE == 0 and seq % FFN_ROW_TILE == 0 and seq % NSA_Q_TILE == 0
    t = batch * seq
    h = x.reshape(t, d)
    mem2d = mem.reshape(batch * mem.shape[1], d)
    for layer in range(DEPTH):
        i = layer // 2
        if layer % 2 == 0:
            qpad, kv0, kvs, gates, u = _in_proj(h, _prep_in_proj_weight(ab_w_in[i]), seq)
            kvc = _cmp_kv(kv0, nsa_cmp_pe[i], nsa_cmp_w1[i], nsa_cmp_w2[i], batch, seq)
            nsa = _nsa_attention(qpad.reshape(batch, seq, Q_COLS), kvc, kvs.reshape(batch, seq, KVS_COLS),
                                 gates.reshape(batch, seq, GATE_COLS), batch, seq)
            h = _pool_out(h, nsa.reshape(t, NSA_WIDTH), u, pool_w[i], pool_scale[i], ab_w_out[i],
                          ln_g[layer, 0], ln_b[layer, 0], seq)
        else:
            h = _sgu(h, sg_w_in[i], sg_norm_g[i], sg_norm_b[i], sg_w_s[i], sg_b_s[i], sg_w_out[i],
                     ln_g[layer, 0], ln_b[layer, 0])
        memkv = _mem_kv(mem2d, mem_wkv[layer]).reshape(batch, mem.shape[1], -1)
        h = _mem_attn(h, memkv, mem_wq[layer], mem_wo[layer], ln_g[layer, 1], ln_b[layer, 1], seq)
        h = _ffn(h, ffn_w_up[layer], ffn_conv_w[layer], ffn_conv_b[layer], ffn_w_down[layer],
                 ln_g[layer, 2], ln_b[layer, 2], seq)
    return h.reshape(batch, seq, d)
```

```python
import functools

import numpy as np
import jax
import jax.numpy as jnp
from jax import lax
from jax.experimental import pallas as pl
from jax.experimental.pallas import tpu as pltpu

F32 = jnp.float32
BF16 = jnp.bfloat16

LANES = 128
SUBLANES = 8
VMEM_LIMIT_BYTES = 56 * 1024 * 1024

D_MODEL = 1024
DEPTH = 2
NSA_HEADS = 8
NSA_KV_HEADS = 2
NSA_GROUP = NSA_HEADS // NSA_KV_HEADS
NSA_HEAD_DIM = 64
NSA_WIDTH = NSA_HEADS * NSA_HEAD_DIM
N_BRANCH = 3
CMP_LEN = 32
CMP_STRIDE = 16
CMP_HIDDEN = 2 * NSA_HEAD_DIM
SEL_BLOCK = 64
SEL_TOP = 16
WINDOW = 512
FORCE_BONUS = 100.0
KV_WIDTH = N_BRANCH * 2 * NSA_KV_HEADS * NSA_HEAD_DIM
GATE_WIDTH = NSA_HEADS * N_BRANCH
POOL_WINDOWS = (2, 4, 8, 16)
POOL_WIDTH = D_MODEL // 2
POOL_GROUP_DIM = POOL_WIDTH // len(POOL_WINDOWS)
SG_CHUNK = 128
SG_GROUPS = 8
SG_WIDTH = D_MODEL
SG_GROUP_DIM = SG_WIDTH // SG_GROUPS
MEM_HEADS = 4
MEM_HEAD_DIM = D_MODEL // MEM_HEADS
D_FF = 256 * ((8 * D_MODEL // 3 + 255) // 256)
CONV_WIDTH = 3
ALPHA = (2 * DEPTH) ** 0.25
LN_EPS = 1e-5
NEG_INF = -1e30

ROW_TILE = 512
SPLIT_ROW_TILE = 1024
ROW_SPLIT = 2
FFN_ROW_TILE = 1024
NSA_Q_TILE = 256
SEL_KEY_CHUNK = 512
FFN_CHUNK = 256
FFN_DOWN_GROUP = 4
POOL_HALO = 16
CONV_HALO = SUBLANES


def _params(n_parallel):
    return pltpu.CompilerParams(
        dimension_semantics=("parallel",) * n_parallel,
        vmem_limit_bytes=VMEM_LIMIT_BYTES,
    )


def _layer_norm(r, g, b):
    mu = jnp.mean(r, axis=-1, keepdims=True)
    d = r - mu
    var = jnp.mean(d * d, axis=-1, keepdims=True)
    return d * lax.rsqrt(var + LN_EPS) * g + b


def _gelu(x):
    c = np.sqrt(2.0 / np.pi).astype(np.float32)
    return 0.5 * x * (1.0 + jnp.tanh(c * (x + 0.044715 * (x * x * x))))


def _dot(a, b):
    return jnp.dot(a, b, preferred_element_type=F32)


def _dot_nt(a, b):
    return lax.dot_general(a, b, (((1,), (1,)), ((), ())), preferred_element_type=F32)


def _row_slices(tm):
    part = tm // ROW_SPLIT
    return [slice(k * part, (k + 1) * part) for k in range(ROW_SPLIT)]


def _run_staggered(streams):
    live = dict(enumerate(streams))
    tick = 0
    while live:
        for k in sorted(live):
            if tick >= k:
                try:
                    next(live[k])
                except StopIteration:
                    del live[k]
        tick += 1


LOG2E = float(np.log2(np.e))
Q_COLS = NSA_HEADS * LANES
KV0_COLS = NSA_KV_HEADS * LANES
SLAB_KSEL, SLAB_VSEL, SLAB_KWIN, SLAB_VWIN = range(4)
N_SLAB = 4
KVS_COLS = N_SLAB * NSA_KV_HEADS * LANES
GATE_COLS = LANES
KV12_COLS = 2 * NSA_KV_HEADS * LANES
IN_PROJ_SPLITS = [int(v) for v in np.cumsum([0, NSA_WIDTH, KV0_COLS, KV12_COLS, GATE_COLS, POOL_WIDTH])]


def _in_proj_kernel(x_ref, w_ref, q_ref, kv0_ref, kvs_ref, gate_ref, u_ref, *, seq):
    tm = x_ref.shape[0]
    ng = NSA_KV_HEADS
    xb = x_ref[...].astype(BF16)

    def cols(k):
        return _dot(xb, w_ref[:, IN_PROJ_SPLITS[k]:IN_PROJ_SPLITS[k + 1]])

    def tile(a, k):
        return a[:, k * LANES:(k + 1) * LANES]

    def put(ref, k, lower_part, fill):
        ref[:, k * LANES:(k + 1) * LANES] = jnp.where(lower, lower_part, fill).astype(ref.dtype)

    pos = (pl.program_id(0) % (seq // tm)) * tm + lax.broadcasted_iota(jnp.int32, (tm, 1), 0)
    lane = lax.broadcasted_iota(jnp.int32, (1, LANES), 1)
    lower = lane < NSA_HEAD_DIM
    onehot = jnp.where(pos // SEL_BLOCK == lane - NSA_HEAD_DIM, 1.0, 0.0)

    q = cols(0) * LOG2E
    for pair in range(NSA_HEADS // 2):
        both = tile(q, pair)
        put(q_ref, 2 * pair, both, 0.0)
        put(q_ref, 2 * pair + 1, pltpu.roll(both, NSA_HEAD_DIM, 1), 0.0)
    kv0_ref[...] = cols(1)
    gate_ref[...] = cols(3)
    u_ref[...] = cols(4)
    kv = cols(2)
    for branch, (kslab, vslab, kfill) in enumerate(((SLAB_KSEL, SLAB_VSEL, onehot), (SLAB_KWIN, SLAB_VWIN, 0.0))):
        for g in range(ng):
            both = tile(kv, branch * ng + g)
            put(kvs_ref, kslab * ng + g, both, kfill)
            put(kvs_ref, vslab * ng + g, pltpu.roll(both, NSA_HEAD_DIM, 1), 1.0)


def _in_proj(x2d, w, seq):
    t = x2d.shape[0]
    n = w.shape[1]
    widths = [Q_COLS, KV0_COLS, KVS_COLS, GATE_COLS, POOL_WIDTH]
    dtypes = [BF16, F32, BF16, F32, F32]
    return pl.pallas_call(
        functools.partial(_in_proj_kernel, seq=seq),
        grid=(t // ROW_TILE,),
        in_specs=[
            pl.BlockSpec((ROW_TILE, D_MODEL), lambda i: (i, 0)),
            pl.BlockSpec((D_MODEL, n), lambda i: (0, 0)),
        ],
        out_specs=[pl.BlockSpec((ROW_TILE, wd), lambda i: (i, 0)) for wd in widths],
        out_shape=[jax.ShapeDtypeStruct((t, wd), dt) for wd, dt in zip(widths, dtypes)],
        compiler_params=_params(1),
        name="in_proj",
    )(x2d, w)


def _prep_in_proj_weight(w):
    d = w.shape[0]
    o_q = NSA_WIDTH
    o_kv = o_q + KV_WIDTH
    o_g = o_kv + GATE_WIDTH
    wq = w[:, :o_q] * NSA_HEAD_DIM ** -0.5
    wkv = w[:, o_q:o_kv].reshape(d, N_BRANCH, 2, NSA_KV_HEADS, NSA_HEAD_DIM)
    wkv = wkv.transpose(0, 1, 3, 2, 4)
    wkv0 = wkv[:, 0].reshape(d, KV0_COLS)
    wkv12 = wkv[:, 1:].reshape(d, KV12_COLS)
    wg = jnp.pad(w[:, o_kv:o_g], ((0, 0), (0, GATE_COLS - GATE_WIDTH)))
    wu = w[:, o_g:]
    return jnp.concatenate([wq, wkv0, wkv12, wg, wu], axis=1).astype(BF16)


def _cmp_kv_kernel(x_ref, pe_ref, w1_ref, w2_ref, o_ref):
    n_chunks = x_ref.shape[1] // CMP_STRIDE
    lane = lax.broadcasted_iota(jnp.int32, (1, LANES), 1)
    half = [None, None]
    for l in range(CMP_STRIDE):
        rows = x_ref[0, pl.ds(l, n_chunks, stride=CMP_STRIDE), :]
        for h in range(2):
            part = _dot((rows + pe_ref[h, l:l + 1, :]).astype(BF16), w1_ref[h, l])
            half[h] = part if half[h] is None else half[h] + part
    hid = _gelu(half[0] + pltpu.roll(half[1], n_chunks - 1, 0))
    for kv in range(2):
        part = _dot(hid[:, kv * CMP_HIDDEN:(kv + 1) * CMP_HIDDEN].astype(BF16), w2_ref[kv])
        if kv == 1:
            part = jnp.where(lane >= NSA_HEAD_DIM, 1.0, part)
        o_ref[0, 0, kv] = part.astype(o_ref.dtype)


def _cmp_kv(kv0, pe, w1, w2, batch, seq):
    n_chunks = seq // CMP_STRIDE
    dh = NSA_HEAD_DIM
    x = kv0.reshape(batch, seq, KV0_COLS)
    pe_t = pe.reshape(2, 2, CMP_STRIDE, dh).transpose(1, 2, 0, 3).reshape(2, CMP_STRIDE, LANES)
    w1r = w1.reshape(2, 2, CMP_STRIDE, dh, CMP_HIDDEN)
    zero = jnp.zeros_like(w1r[0])
    w1_t = jnp.concatenate([jnp.concatenate([w1r[0], zero], axis=-1),
                            jnp.concatenate([zero, w1r[1]], axis=-1)], axis=-2).astype(BF16)
    w2p = jnp.pad(w2, ((0, 0), (0, 0), (0, LANES - dh))).astype(BF16)
    return pl.pallas_call(
        _cmp_kv_kernel,
        grid=(batch, NSA_KV_HEADS),
        in_specs=[
            pl.BlockSpec((1, seq, LANES), lambda b, g: (b, 0, g)),
            pl.BlockSpec((2, CMP_STRIDE, LANES), lambda b, g: (0, 0, 0)),
            pl.BlockSpec((2, CMP_STRIDE, LANES, 2 * CMP_HIDDEN), lambda b, g: (0, 0, 0, 0)),
            pl.BlockSpec((2, CMP_HIDDEN, LANES), lambda b, g: (0, 0, 0)),
        ],
        out_specs=pl.BlockSpec((1, 1, 2, n_chunks, LANES), lambda b, g: (b, g, 0, 0, 0)),
        out_shape=jax.ShapeDtypeStruct((batch, NSA_KV_HEADS, 2, n_chunks, LANES), BF16),
        compiler_params=_params(2),
        name="cmp_kv",
    )(x, pe_t, w1_t, w2p)


def _split3(x):
    a = x.astype(BF16)
    r = x - a.astype(F32)
    b = r.astype(BF16)
    c = (r - b.astype(F32)).astype(BF16)
    return a, b, c


def _lane_tiles(s):
    return [s[:, i * LANES:(i + 1) * LANES] for i in range(s.shape[1] // LANES)]


def _row_max_bcast(tiles):
    mt = tiles[0]
    for t in tiles[1:]:
        mt = jnp.maximum(mt, t)
    return jnp.broadcast_to(jnp.max(mt, axis=-1, keepdims=True), mt.shape)


def _nsa_group(g, t0, q_ref, cmp_ref, kvs_ref, gate_ref, ovl_ref, exp_ref, o_ref,
               sc_scr, gate_scr, m_scr, acc_scr, out_scr, seq):
    qt = NSA_Q_TILE
    rr = NSA_GROUP
    dh = NSA_HEAD_DIM
    m_rows = rr * qt
    n_cmp_rows = cmp_ref.shape[3]
    n_sel = seq // SEL_BLOCK
    win_keys = WINDOW + qt

    def slab(k):
        c0 = (k * NSA_KV_HEADS + g) * LANES
        return kvs_ref.at[:, :, c0:c0 + LANES]

    ksel_ref, vsel_ref, kwin_ref, vwin_ref = (slab(k) for k in (SLAB_KSEL, SLAB_VSEL, SLAB_KWIN, SLAB_VWIN))
    q_ref = q_ref.at[:, :, g * rr * LANES:(g + 1) * rr * LANES]
    cmp_ref = cmp_ref.at[:, g:g + 1]
    o_ref = o_ref.at[:, :, g * rr * dh:(g + 1) * rr * dh]
    sc_scr, gate_scr, m_scr, acc_scr, out_scr = (r.at[g] for r in (sc_scr, gate_scr, m_scr, acc_scr, out_scr))

    qb = q_ref[0]
    qp = jnp.concatenate([qb[:, r * LANES:(r + 1) * LANES] for r in range(rr)], axis=0)
    tq = t0 + lax.broadcasted_iota(jnp.int32, (qt, 1), 0)
    lane = lax.broadcasted_iota(jnp.int32, (1, LANES), 1)
    lower = lane < dh
    gate = jax.nn.sigmoid(gate_ref[0])

    for r in range(rr):
        for branch in range(N_BRANCH):
            c = (g * rr + r) * N_BRANCH + branch
            gate_scr[r * N_BRANCH + branch] = jnp.broadcast_to(gate[:, c:c + 1], (qt, LANES))

    def normalize(acc):
        inv = jnp.where(lower, 1.0 / pltpu.roll(acc, dh, 1), 0.0)
        return acc * inv

    kcmp = cmp_ref[0, 0, 0]
    vcmp = cmp_ref[0, 0, 1]
    s = _dot_nt(qp, kcmp).reshape(rr, qt, n_cmp_rows)
    n_idx = lax.broadcasted_iota(jnp.int32, (1, n_cmp_rows), 1)
    n_cmp = (seq - CMP_LEN) // CMP_STRIDE + 1
    valid = (n_idx * CMP_STRIDE + (CMP_LEN - 1) <= tq) & (n_idx < n_cmp)
    s = s + jnp.where(valid, 0.0, NEG_INF)[None]
    e = jnp.exp2(s - jnp.max(s, axis=-1, keepdims=True))
    inv = 1.0 / jnp.sum(e, axis=-1, keepdims=True)
    has_cmp = (tq >= CMP_LEN - 1).astype(F32)
    p = e * (inv * has_cmp[None])
    out_scr[...] = _dot(p.reshape(m_rows, n_cmp_rows).astype(BF16), vcmp)
    yield

    def window_branch():
        w0 = pl.multiple_of(jnp.maximum(t0 - WINDOW, 0), qt)
        kw = kwin_ref[0, pl.ds(w0, win_keys), :]
        vw = vwin_ref[0, pl.ds(w0, win_keys), :]
        wpos = w0 + lax.broadcasted_iota(jnp.int32, (1, win_keys), 1)
        okw = (wpos <= tq) & (wpos > tq - WINDOW)
        s = _dot_nt(qp, kw).reshape(rr, qt, win_keys) + jnp.where(okw, 0.0, NEG_INF)[None]
        tiles = _lane_tiles(s.reshape(m_rows, win_keys))
        m_b = _row_max_bcast(tiles)
        pw = jnp.concatenate([jnp.exp2(t - m_b) for t in tiles], axis=1).astype(BF16)
        return normalize(_dot(pw, vw))

    psum = p[0]
    for r in range(1, rr):
        psum = psum + p[r]
    ovl_t = ovl_ref[...]
    imp_t = None
    for term in _split3(psum):
        part = _dot_nt(ovl_t, term)
        imp_t = part if imp_t is None else imp_t + part
    imp = imp_t[dh:, :]
    j_s = lax.broadcasted_iota(jnp.int32, (dh, 1), 0)
    tq_l = t0 + lax.broadcasted_iota(jnp.int32, (1, qt), 1)
    cur = tq_l // SEL_BLOCK
    blk_ok = j_s * SEL_BLOCK <= tq_l
    forced = (j_s == 0) | (j_s == cur) | (j_s == cur - 1)
    score = jnp.where(blk_ok, imp + jnp.where(forced, FORCE_BONUS, 0.0), NEG_INF)
    score = jnp.where(j_s < n_sel, score, -jnp.inf)
    sc_scr[...] = score
    yield
    n_grp = dh // SUBLANES
    grp = [score[k * SUBLANES:(k + 1) * SUBLANES] for k in range(n_grp)]
    rank = [jnp.zeros((SUBLANES, qt), F32) for _ in range(n_grp)]
    sub = lax.broadcasted_iota(jnp.int32, (SUBLANES, 1), 0)
    for jp in range(dh):
        row = jnp.broadcast_to(sc_scr[jp:jp + 1, :], (SUBLANES, qt))
        kj = jp // SUBLANES
        for k in range(n_grp):
            if k > kj:
                rank[k] = rank[k] + jnp.where(row >= grp[k], 1.0, 0.0)
            elif k < kj:
                rank[k] = rank[k] + jnp.where(row > grp[k], 1.0, 0.0)
            else:
                ge = jnp.where(row >= grp[k], 1.0, 0.0)
                gt = jnp.where(row > grp[k], 1.0, 0.0)
                rank[k] = rank[k] + jnp.where(sub > jp % SUBLANES, ge, gt)
    n_top = float(min(SEL_TOP, n_sel))
    sel_t = jnp.concatenate(rank, axis=0) < n_top
    sel_t = jnp.where(sel_t & blk_ok, 1.0, 0.0)
    sel = jnp.concatenate([jnp.zeros((dh, qt), F32), sel_t], axis=0).T
    yield

    m_scr[...] = jnp.full(m_scr.shape, NEG_INF, F32)
    acc_scr[...] = jnp.zeros(acc_scr.shape, F32)

    def online_step(tiles, v_slab):
        m_old = m_scr[...]
        m_new = jnp.maximum(m_old, _row_max_bcast(tiles))
        alpha = jnp.exp2(m_old - m_new)
        pexp = jnp.concatenate([jnp.exp2(t - m_new) for t in tiles], axis=1).astype(BF16)
        return m_new, alpha * acc_scr[...] + _dot(pexp, v_slab)

    def diagonal():
        d0 = pl.multiple_of(t0, qt)
        chosen = _dot(sel.astype(BF16), exp_ref[:, pl.ds(d0, qt)])
        kpos = t0 + lax.broadcasted_iota(jnp.int32, (1, qt), 1)
        okd = (chosen > 0.5) & (kpos <= tq)
        kd = ksel_ref[0, pl.ds(d0, qt), :]
        vd = vsel_ref[0, pl.ds(d0, qt), :]
        s = (_dot_nt(qp, kd).reshape(rr, qt, qt) + jnp.where(okd, 0.0, NEG_INF)[None]).reshape(m_rows, qt)
        return normalize(online_step(_lane_tiles(s), vd)[1])

    blk_lane = lane - dh
    bias = jnp.where((sel > 0.5) & (blk_lane * SEL_BLOCK < t0), 0.0, NEG_INF).astype(BF16)
    q_aug = jnp.concatenate([jnp.where(lower, qp[r * qt:(r + 1) * qt], bias) for r in range(rr)], axis=0)
    kc = SEL_KEY_CHUNK

    def step(c):
        k0 = pl.multiple_of(c * kc, kc)
        ks = ksel_ref[0, pl.ds(k0, kc), :]
        vs = vsel_ref[0, pl.ds(k0, kc), :]
        m_scr[...], acc_scr[...] = online_step(_lane_tiles(_dot_nt(q_aug, ks)), vs)

    def finish(o_sel, o_win):
        branches = (out_scr[...], o_sel, o_win)
        heads = []
        for r in range(rr):
            rows = slice(r * qt, (r + 1) * qt)
            comb = gate_scr[r * N_BRANCH] * branches[0][rows]
            for branch in range(1, N_BRANCH):
                comb = comb + gate_scr[r * N_BRANCH + branch] * branches[branch][rows]
            heads.append(comb)
        for pair in range(rr // 2):
            odd = pltpu.roll(heads[2 * pair + 1], dh, 1)
            o_ref[0, :, pair * LANES:(pair + 1) * LANES] = jnp.where(lower, heads[2 * pair], odd).astype(o_ref.dtype)

    return step, window_branch, diagonal, finish


def _nsa_kernel(*refs, seq):
    t0 = pl.program_id(1) * NSA_Q_TILE
    pending = {g: _nsa_group(g, t0, *refs, seq) for g in range(NSA_KV_HEADS)}
    done = {}
    while pending:
        for g in sorted(pending):
            try:
                next(pending[g])
            except StopIteration as stop:
                done[g] = stop.value
                del pending[g]
    groups = [done[g] for g in sorted(done)]

    def sel_step(c, carry):
        for grp in groups:
            grp[0](c)
        return carry

    lax.fori_loop(0, (t0 + SEL_KEY_CHUNK - 1) // SEL_KEY_CHUNK, sel_step, 0)
    o_win = [grp[1]() for grp in groups]
    o_sel = [grp[2]() for grp in groups]
    for grp, sel_out, win_out in zip(groups, o_sel, o_win):
        grp[3](sel_out, win_out)


def _nsa_attention(qpad, cmp_slabs, kvs, gates, batch, seq):
    qt = NSA_Q_TILE
    dh = NSA_HEAD_DIM
    n_sel = seq // SEL_BLOCK
    n_chunks = seq // CMP_STRIDE
    assert n_sel <= dh and seq % SEL_KEY_CHUNK == 0 and seq >= WINDOW + qt and SEL_KEY_CHUNK % qt == 0
    n_cmp = (seq - CMP_LEN) // CMP_STRIDE + 1
    cmp_start = np.arange(n_chunks) * CMP_STRIDE
    blk = np.arange(LANES) - dh
    overlap_t = ((cmp_start[None, :] < blk[:, None] * SEL_BLOCK + SEL_BLOCK)
                 & (cmp_start[None, :] + CMP_LEN > blk[:, None] * SEL_BLOCK)
                 & (np.arange(n_chunks)[None, :] < n_cmp) & (blk[:, None] >= 0) & (blk[:, None] < n_sel))
    overlap_t = jnp.asarray(overlap_t.astype(np.float32), BF16)
    expand = (np.arange(seq)[None, :] // SEL_BLOCK) == blk[:, None]
    expand = jnp.asarray(expand.astype(np.float32), BF16)
    m_rows = NSA_GROUP * qt
    ng = NSA_KV_HEADS
    return pl.pallas_call(
        functools.partial(_nsa_kernel, seq=seq),
        grid=(batch, seq // qt),
        in_specs=[
            pl.BlockSpec((1, qt, Q_COLS), lambda b, i: (b, i, 0)),
            pl.BlockSpec((1, ng, 2, n_chunks, LANES), lambda b, i: (b, 0, 0, 0, 0)),
            pl.BlockSpec((1, seq, KVS_COLS), lambda b, i: (b, 0, 0)),
            pl.BlockSpec((1, qt, GATE_COLS), lambda b, i: (b, i, 0)),
            pl.BlockSpec((LANES, n_chunks), lambda b, i: (0, 0)),
            pl.BlockSpec((LANES, seq), lambda b, i: (0, 0)),
        ],
        out_specs=pl.BlockSpec((1, qt, NSA_WIDTH), lambda b, i: (b, i, 0)),
        out_shape=jax.ShapeDtypeStruct((batch, seq, NSA_WIDTH), BF16),
        scratch_shapes=[
            pltpu.VMEM((ng, dh, qt), F32),
            pltpu.VMEM((ng, NSA_GROUP * N_BRANCH, qt, LANES), F32),
            pltpu.VMEM((ng, m_rows, LANES), F32),
            pltpu.VMEM((ng, m_rows, LANES), F32),
            pltpu.VMEM((ng, m_rows, LANES), F32),
        ],
        compiler_params=_params(2),
        name="nsa_attention",
    )(qpad, cmp_slabs, kvs, gates, overlap_t, expand)


def _pool_out_rows(rows, x_ref, nsa_ref, u_ref, uh_ref, pw_ref, ps_ref, wo_ref, g_ref, b_ref, o_ref, seq):
    tm = x_ref.shape[0]
    n_rows = rows.stop - rows.start
    i = pl.program_id(0)
    tiles_per_seq = seq // tm
    t_seq = (i % tiles_per_seq) * tm + rows.start + lax.broadcasted_iota(jnp.int32, (n_rows, 1), 0)
    u = u_ref[rows, :]
    if rows.start == 0:
        halo = jnp.where((i % tiles_per_seq) == 0, 0.0, uh_ref[...])
    else:
        halo = u_ref[rows.start - POOL_HALO:rows.start, :]
    z = jnp.concatenate([halo, u], axis=0)
    gd = POOL_GROUP_DIM
    y = jnp.dot(nsa_ref[rows, :], wo_ref[0:NSA_WIDTH, :], preferred_element_type=F32)
    yield
    mixed = []
    for gi, w in enumerate(POOL_WINDOWS):
        run = z[:, gi * gd:(gi + 1) * gd]
        span = 1
        while span < w:
            run = run + pltpu.roll(run, span, 0)
            span *= 2
        cnt = jnp.minimum(t_seq + 1, w).astype(F32)
        pooled = run[POOL_HALO:, :] / cnt - u[:, gi * gd:(gi + 1) * gd]
        mixed.append((_dot(pooled.astype(BF16), pw_ref[gi]) * ps_ref[:, gi * gd:(gi + 1) * gd]).astype(BF16))
    yield
    y = y + _dot(jnp.concatenate(mixed, axis=1), wo_ref[NSA_WIDTH:, :])
    yield
    o_ref[rows, :] = _layer_norm(ALPHA * x_ref[rows, :] + y, g_ref[...], b_ref[...])


def _pool_out_kernel(*refs, seq):
    _run_staggered([_pool_out_rows(rows, *refs, seq) for rows in _row_slices(refs[0].shape[0])])


def _pool_out(x2d, nsa2d, u2d, pool_w, pool_scale, w_out, ln_g, ln_b, seq):
    t = x2d.shape[0]
    tm = SPLIT_ROW_TILE
    hb = tm // POOL_HALO
    return pl.pallas_call(
        functools.partial(_pool_out_kernel, seq=seq),
        grid=(t // tm,),
        in_specs=[
            pl.BlockSpec((tm, D_MODEL), lambda i: (i, 0)),
            pl.BlockSpec((tm, NSA_WIDTH), lambda i: (i, 0)),
            pl.BlockSpec((tm, POOL_WIDTH), lambda i: (i, 0)),
            pl.BlockSpec((POOL_HALO, POOL_WIDTH), lambda i: (jnp.maximum(i * hb - 1, 0), 0)),
            pl.BlockSpec(pool_w.shape, lambda i: (0, 0, 0)),
            pl.BlockSpec((1, POOL_WIDTH), lambda i: (0, 0)),
            pl.BlockSpec(w_out.shape, lambda i: (0, 0)),
            pl.BlockSpec((1, D_MODEL), lambda i: (0, 0)),
            pl.BlockSpec((1, D_MODEL), lambda i: (0, 0)),
        ],
        out_specs=pl.BlockSpec((tm, D_MODEL), lambda i: (i, 0)),
        out_shape=jax.ShapeDtypeStruct((t, D_MODEL), F32),
        compiler_params=_params(1),
        name="pool_out_ln",
    )(x2d, nsa2d, u2d, u2d, pool_w.astype(BF16), pool_scale.reshape(1, -1), w_out.astype(BF16),
      ln_g.reshape(1, -1), ln_b.reshape(1, -1))


def _sgu_rows(rows, x_ref, win_ref, ng_ref, nb_ref, ws_ref, bs_ref, wo_ref, g_ref, b_ref, o_ref, sv_scr):
    n_rows = rows.stop - rows.start
    n_chunk = n_rows // SG_CHUNK
    gd = SG_GROUP_DIM
    x = x_ref[rows, :]
    xb = x.astype(BF16)
    hu = _dot(xb, win_ref[:, 0:SG_WIDTH])
    hv = _dot(xb, win_ref[:, SG_WIDTH:2 * SG_WIDTH])
    yield
    zu = _gelu(hu)
    v = _layer_norm(_gelu(hv), ng_ref[...], nb_ref[...]).astype(BF16)
    yield
    row = lax.broadcasted_iota(jnp.int32, (SG_CHUNK, SG_CHUNK), 0)
    col = lax.broadcasted_iota(jnp.int32, (SG_CHUNK, SG_CHUNK), 1)
    causal = col <= row
    for h in range(SG_GROUPS):
        w = jnp.where(causal, ws_ref[h], 0.0).astype(BF16)
        rhs = jnp.concatenate(
            [v[c * SG_CHUNK:(c + 1) * SG_CHUNK, h * gd:(h + 1) * gd] for c in range(n_chunk)], axis=1)
        res = _dot(w, rhs)
        for c in range(n_chunk):
            r0 = rows.start + c * SG_CHUNK
            sv_scr[r0:r0 + SG_CHUNK, h * gd:(h + 1) * gd] = res[:, c * gd:(c + 1) * gd]
    bias = bs_ref[...]
    gated = jnp.concatenate(
        [zu[c * SG_CHUNK:(c + 1) * SG_CHUNK, :]
         * (sv_scr[rows.start + c * SG_CHUNK:rows.start + (c + 1) * SG_CHUNK, :] + bias) for c in range(n_chunk)],
        axis=0).astype(BF16)
    yield
    y = _dot(gated, wo_ref[...])
    yield
    o_ref[rows, :] = _layer_norm(ALPHA * x + y, g_ref[...], b_ref[...])


def _sgu_kernel(*refs):
    _run_staggered([_sgu_rows(rows, *refs) for rows in _row_slices(refs[0].shape[0])])


def _sgu(x2d, w_in, norm_g, norm_b, w_s, b_s, w_out, ln_g, ln_b):
    t = x2d.shape[0]
    tm = SPLIT_ROW_TILE
    bias = jnp.repeat(b_s.T, SG_GROUP_DIM, axis=1)
    vec = lambda a: a.reshape(1, -1)
    const2 = lambda i: (0, 0)
    return pl.pallas_call(
        _sgu_kernel,
        grid=(t // tm,),
        in_specs=[
            pl.BlockSpec((tm, D_MODEL), lambda i: (i, 0)),
            pl.BlockSpec(w_in.shape, const2),
            pl.BlockSpec((1, SG_WIDTH), const2),
            pl.BlockSpec((1, SG_WIDTH), const2),
            pl.BlockSpec(w_s.shape, lambda i: (0, 0, 0)),
            pl.BlockSpec(bias.shape, const2),
            pl.BlockSpec(w_out.shape, const2),
            pl.BlockSpec((1, D_MODEL), const2),
            pl.BlockSpec((1, D_MODEL), const2),
        ],
        out_specs=pl.BlockSpec((tm, D_MODEL), lambda i: (i, 0)),
        out_shape=jax.ShapeDtypeStruct((t, D_MODEL), F32),
        scratch_shapes=[pltpu.VMEM((tm, SG_WIDTH), F32)],
        compiler_params=_params(1),
        name="sgu_out_ln",
    )(x2d, w_in.astype(BF16), vec(norm_g), vec(norm_b), w_s, bias, w_out.astype(BF16), vec(ln_g), vec(ln_b))


def _mem_kv_kernel(m_ref, w_ref, o_ref):
    o_ref[...] = _dot(m_ref[...].astype(BF16), w_ref[...]).astype(o_ref.dtype)


def _mem_kv(mem2d, wkv):
    rows = mem2d.shape[0]
    n = wkv.shape[1]
    tm = min(ROW_TILE, rows)
    return pl.pallas_call(
        _mem_kv_kernel,
        grid=(rows // tm,),
        in_specs=[pl.BlockSpec((tm, D_MODEL), lambda i: (i, 0)), pl.BlockSpec(wkv.shape, lambda i: (0, 0))],
        out_specs=pl.BlockSpec((tm, n), lambda i: (i, 0)),
        out_shape=jax.ShapeDtypeStruct((rows, n), BF16),
        compiler_params=_params(1),
        name="mem_kv",
    )(mem2d, wkv.astype(BF16))


def _mem_attn_rows(rows, x_ref, kv_ref, wq_ref, wo_ref, g_ref, b_ref, o_ref):
    x = x_ref[rows, :]
    q = _dot(x.astype(BF16), wq_ref[...]).astype(BF16)
    yield
    hd = MEM_HEAD_DIM
    width = MEM_HEADS * hd
    outs = []
    for h in range(MEM_HEADS):
        k = kv_ref[0, :, h * hd:(h + 1) * hd]
        v = kv_ref[0, :, width + h * hd:width + (h + 1) * hd]
        s = _dot_nt(q[:, h * hd:(h + 1) * hd], k)
        e = jnp.exp(s - jnp.max(s, axis=-1, keepdims=True))
        p = e * (1.0 / jnp.sum(e, axis=-1, keepdims=True))
        outs.append(_dot(p.astype(BF16), v).astype(BF16))
    yield
    y = _dot(jnp.concatenate(outs, axis=1), wo_ref[...])
    yield
    o_ref[rows, :] = _layer_norm(ALPHA * x + y, g_ref[...], b_ref[...])


def _mem_attn_kernel(*refs):
    _run_staggered([_mem_attn_rows(rows, *refs) for rows in _row_slices(refs[0].shape[0])])


def _mem_attn(x2d, memkv, wq, wo, ln_g, ln_b, seq):
    t = x2d.shape[0]
    tm = SPLIT_ROW_TILE
    tiles_per_seq = seq // tm
    mem_len, kv_cols = memkv.shape[1], memkv.shape[2]
    const2 = lambda i: (0, 0)
    wq_s = (wq * MEM_HEAD_DIM ** -0.5).astype(BF16)
    return pl.pallas_call(
        _mem_attn_kernel,
        grid=(t // tm,),
        in_specs=[
            pl.BlockSpec((tm, D_MODEL), lambda i: (i, 0)),
            pl.BlockSpec((1, mem_len, kv_cols), lambda i: (i // tiles_per_seq, 0, 0)),
            pl.BlockSpec(wq.shape, const2),
            pl.BlockSpec(wo.shape, const2),
            pl.BlockSpec((1, D_MODEL), const2),
            pl.BlockSpec((1, D_MODEL), const2),
        ],
        out_specs=pl.BlockSpec((tm, D_MODEL), lambda i: (i, 0)),
        out_shape=jax.ShapeDtypeStruct((t, D_MODEL), F32),
        compiler_params=_params(1),
        name="mem_attn_ln",
    )(x2d, memkv, wq_s, wo.astype(BF16), ln_g.reshape(1, -1), ln_b.reshape(1, -1))


def _ffn_kernel(x_ref, xh_ref, wup_ref, cw_ref, cb_ref, wd_ref, g_ref, b_ref, o_ref, *, seq):
    tm = x_ref.shape[0]
    i = pl.program_id(0)
    first = (i % (seq // tm)) == 0
    x = x_ref[...]
    halo = jnp.where(first, 0.0, xh_ref[...])
    xcat = jnp.concatenate([halo, x], axis=0).astype(BF16)
    n_steps = D_FF // FFN_CHUNK

    def cols(ref, c):
        lo = c * FFN_CHUNK
        return jnp.concatenate([ref[:, lo:lo + FFN_CHUNK], ref[:, D_FF + lo:D_FF + lo + FFN_CHUNK]], axis=1)

    def up(c):
        lo = c * FFN_CHUNK
        return jnp.concatenate([_dot(xcat, wup_ref[:, lo:lo + FFN_CHUNK]),
                                _dot(xcat, wup_ref[:, D_FF + lo:D_FF + lo + FFN_CHUNK])], axis=1)

    def shift_rows(v):
        blocks = v.reshape(v.shape[0] // SUBLANES, SUBLANES, v.shape[1])
        rot = pltpu.roll(blocks, 1, 1)
        prev = jnp.concatenate([rot[-1:], rot[:-1]], axis=0)
        sub = lax.broadcasted_iota(jnp.int32, (1, SUBLANES, 1), 1)
        return jnp.where(sub == 0, prev, rot).reshape(v.shape)

    y = None
    acts = []
    h_next = up(0)
    for c in range(n_steps):
        h = h_next
        if c + 1 < n_steps:
            h_next = up(c + 1)
        cw = cols(cw_ref, c)
        out = cw[0:1, :] * h
        for j in range(1, CONV_WIDTH):
            out = cw[j:j + 1, :] * h + shift_rows(out)
        out = out[CONV_HALO:, :] + cols(cb_ref, c)
        a, gt = out[:, :FFN_CHUNK], out[:, FFN_CHUNK:]
        acts.append((a * (gt * jax.nn.sigmoid(gt))).astype(BF16))
        if len(acts) == FFN_DOWN_GROUP or c + 1 == n_steps:
            r0 = (c + 1 - len(acts)) * FFN_CHUNK
            part = _dot(jnp.concatenate(acts, axis=1), wd_ref[r0:(c + 1) * FFN_CHUNK, :])
            y = part if y is None else y + part
            acts = []
    o_ref[...] = _layer_norm(ALPHA * x + y, g_ref[...], b_ref[...])


def _ffn(x2d, w_up, conv_w, conv_b, w_down, ln_g, ln_b, seq):
    t = x2d.shape[0]
    tm = FFN_ROW_TILE
    hb = tm // CONV_HALO
    const2 = lambda i: (0, 0)
    return pl.pallas_call(
        functools.partial(_ffn_kernel, seq=seq),
        grid=(t // tm,),
        in_specs=[
            pl.BlockSpec((tm, D_MODEL), lambda i: (i, 0)),
            pl.BlockSpec((CONV_HALO, D_MODEL), lambda i: (jnp.maximum(i * hb - 1, 0), 0)),
            pl.BlockSpec(w_up.shape, const2, pipeline_mode=pl.Buffered(1)),
            pl.BlockSpec(conv_w.shape, const2),
            pl.BlockSpec((1, 2 * D_FF), const2),
            pl.BlockSpec(w_down.shape, const2, pipeline_mode=pl.Buffered(1)),
            pl.BlockSpec((1, D_MODEL), const2),
            pl.BlockSpec((1, D_MODEL), const2),
        ],
        out_specs=pl.BlockSpec((tm, D_MODEL), lambda i: (i, 0)),
        out_shape=jax.ShapeDtypeStruct((t, D_MODEL), F32),
        compiler_params=_params(1),
        name="conv_ffn_ln",
    )(x2d, x2d, w_up.astype(BF16), conv_w, conv_b.reshape(1, -1), w_down.astype(BF16),
      ln_g.reshape(1, -1), ln_b.reshape(1, -1))


def kernel(x, mem, ab_w_in, nsa_cmp_pe, nsa_cmp_w1, nsa_cmp_w2, pool_w, pool_scale, ab_w_out,
           sg_w_in, sg_norm_g, sg_norm_b, sg_w_s, sg_b_s, sg_w_out, ln_g, ln_b,
           mem_wq, mem_wkv, mem_wo, ffn_w_up, ffn_conv_w, ffn_conv_b, ffn_w_down):
    batch, seq, d = x.shape
    assert d == D_MODEL and seq % NSA_Q_TILE == 0
    assert all(seq % tile == 0 for tile in (ROW_TILE, SPLIT_ROW_TILE, FFN_ROW_TILE))
    t = batch * seq
    h = x.reshape(t, d)
    mem2d = mem.reshape(batch * mem.shape[1], d)
    for layer in range(DEPTH):
        i = layer // 2
        if layer % 2 == 0:
            qpad, kv0, kvs, gates, u = _in_proj(h, _prep_in_proj_weight(ab_w_in[i]), seq)
            kvc = _cmp_kv(kv0, nsa_cmp_pe[i], nsa_cmp_w1[i], nsa_cmp_w2[i], batch, seq)
            nsa = _nsa_attention(qpad.reshape(batch, seq, Q_COLS), kvc, kvs.reshape(batch, seq, KVS_COLS),
                                 gates.reshape(batch, seq, GATE_COLS), batch, seq)
            h = _pool_out(h, nsa.reshape(t, NSA_WIDTH), u, pool_w[i], pool_scale[i], ab_w_out[i],
                          ln_g[layer, 0], ln_b[layer, 0], seq)
        else:
            h = _sgu(h, sg_w_in[i], sg_norm_g[i], sg_norm_b[i], sg_w_s[i], sg_b_s[i], sg_w_out[i],
                     ln_g[layer, 0], ln_b[layer, 0])
        memkv = _mem_kv(mem2d, mem_wkv[layer]).reshape(batch, mem.shape[1], -1)
        h = _mem_attn(h, memkv, mem_wq[layer], mem_wo[layer], ln_g[layer, 1], ln_b[layer, 1], seq)
        h = _ffn(h, ffn_w_up[layer], ffn_conv_w[layer], ffn_conv_b[layer], ffn_w_down[layer],
                 ln_g[layer, 2], ln_b[layer, 2], seq)
    return h.reshape(batch, seq, d)
```

```python
import functools

import numpy as np
import jax
import jax.numpy as jnp
from jax import lax
from jax.experimental import pallas as pl
from jax.experimental.pallas import tpu as pltpu

F32 = jnp.float32
BF16 = jnp.bfloat16

LANES = 128
SUBLANES = 8
VMEM_LIMIT_BYTES = 56 * 1024 * 1024

D_MODEL = 1024
DEPTH = 2
NSA_HEADS = 8
NSA_KV_HEADS = 2
NSA_GROUP = NSA_HEADS // NSA_KV_HEADS
NSA_HEAD_DIM = 64
NSA_WIDTH = NSA_HEADS * NSA_HEAD_DIM
N_BRANCH = 3
CMP_LEN = 32
CMP_STRIDE = 16
CMP_HIDDEN = 2 * NSA_HEAD_DIM
SEL_BLOCK = 64
SEL_TOP = 16
WINDOW = 512
FORCE_BONUS = 100.0
KV_WIDTH = N_BRANCH * 2 * NSA_KV_HEADS * NSA_HEAD_DIM
GATE_WIDTH = NSA_HEADS * N_BRANCH
POOL_WINDOWS = (2, 4, 8, 16)
POOL_WIDTH = D_MODEL // 2
POOL_GROUP_DIM = POOL_WIDTH // len(POOL_WINDOWS)
SG_CHUNK = 128
SG_GROUPS = 8
SG_WIDTH = D_MODEL
SG_GROUP_DIM = SG_WIDTH // SG_GROUPS
MEM_HEADS = 4
MEM_HEAD_DIM = D_MODEL // MEM_HEADS
D_FF = 256 * ((8 * D_MODEL // 3 + 255) // 256)
CONV_WIDTH = 3
ALPHA = (2 * DEPTH) ** 0.25
LN_EPS = 1e-5
NEG_INF = -1e30

ROW_TILE = 512
SPLIT_ROW_TILE = 1024
ROW_SPLIT = 2
FFN_ROW_TILE = 1024
NSA_Q_TILE = 256
SEL_KEY_CHUNK = 512
FFN_CHUNK = 256
FFN_DOWN_GROUP = 11
POOL_HALO = 16
CONV_HALO = SUBLANES


def _params(n_parallel):
    return pltpu.CompilerParams(
        dimension_semantics=("parallel",) * n_parallel,
        vmem_limit_bytes=VMEM_LIMIT_BYTES,
    )


def _layer_norm(r, g, b):
    mu = jnp.mean(r, axis=-1, keepdims=True)
    d = r - mu
    var = jnp.mean(d * d, axis=-1, keepdims=True)
    return d * lax.rsqrt(var + LN_EPS) * g + b


def _gelu(x):
    c = np.sqrt(2.0 / np.pi).astype(np.float32)
    return 0.5 * x * (1.0 + jnp.tanh(c * (x + 0.044715 * (x * x * x))))


def _dot(a, b):
    return jnp.dot(a, b, preferred_element_type=F32)


def _dot_nt(a, b):
    return lax.dot_general(a, b, (((1,), (1,)), ((), ())), preferred_element_type=F32)


def _row_slices(tm):
    part = tm // ROW_SPLIT
    return [slice(k * part, (k + 1) * part) for k in range(ROW_SPLIT)]


def _run_staggered(streams):
    live = dict(enumerate(streams))
    tick = 0
    while live:
        for k in sorted(live):
            if tick >= k:
                try:
                    next(live[k])
                except StopIteration:
                    del live[k]
        tick += 1


LOG2E = float(np.log2(np.e))
Q_COLS = NSA_HEADS * LANES
KV0_COLS = NSA_KV_HEADS * LANES
SLAB_KSEL, SLAB_VSEL, SLAB_KWIN, SLAB_VWIN = range(4)
N_SLAB = 4
KVS_COLS = N_SLAB * NSA_KV_HEADS * LANES
GATE_COLS = LANES
KV12_COLS = 2 * NSA_KV_HEADS * LANES
IN_PROJ_SPLITS = [int(v) for v in np.cumsum([0, NSA_WIDTH, KV0_COLS, KV12_COLS, GATE_COLS, POOL_WIDTH])]


def _in_proj_kernel(x_ref, w_ref, q_ref, kv0_ref, kvs_ref, gate_ref, u_ref, *, seq):
    tm = x_ref.shape[0]
    ng = NSA_KV_HEADS
    xb = x_ref[...].astype(BF16)

    def cols(k):
        return _dot(xb, w_ref[:, IN_PROJ_SPLITS[k]:IN_PROJ_SPLITS[k + 1]])

    def tile(a, k):
        return a[:, k * LANES:(k + 1) * LANES]

    def put(ref, k, lower_part, fill):
        ref[:, k * LANES:(k + 1) * LANES] = jnp.where(lower, lower_part, fill).astype(ref.dtype)

    pos = (pl.program_id(0) % (seq // tm)) * tm + lax.broadcasted_iota(jnp.int32, (tm, 1), 0)
    lane = lax.broadcasted_iota(jnp.int32, (1, LANES), 1)
    lower = lane < NSA_HEAD_DIM
    onehot = jnp.where(pos // SEL_BLOCK == lane - NSA_HEAD_DIM, 1.0, 0.0)

    q = cols(0) * LOG2E
    for pair in range(NSA_HEADS // 2):
        both = tile(q, pair)
        put(q_ref, 2 * pair, both, 0.0)
        put(q_ref, 2 * pair + 1, pltpu.roll(both, NSA_HEAD_DIM, 1), 0.0)
    kv0_ref[...] = cols(1)
    gate_ref[...] = cols(3)
    u_ref[...] = cols(4)
    kv = cols(2)
    for branch, (kslab, vslab, kfill) in enumerate(((SLAB_KSEL, SLAB_VSEL, onehot), (SLAB_KWIN, SLAB_VWIN, 0.0))):
        for g in range(ng):
            both = tile(kv, branch * ng + g)
            put(kvs_ref, kslab * ng + g, both, kfill)
            put(kvs_ref, vslab * ng + g, pltpu.roll(both, NSA_HEAD_DIM, 1), 1.0)


def _in_proj(x2d, w, seq):
    t = x2d.shape[0]
    n = w.shape[1]
    widths = [Q_COLS, KV0_COLS, KVS_COLS, GATE_COLS, POOL_WIDTH]
    dtypes = [BF16, F32, BF16, F32, F32]
    return pl.pallas_call(
        functools.partial(_in_proj_kernel, seq=seq),
        grid=(t // ROW_TILE,),
        in_specs=[
            pl.BlockSpec((ROW_TILE, D_MODEL), lambda i: (i, 0)),
            pl.BlockSpec((D_MODEL, n), lambda i: (0, 0)),
        ],
        out_specs=[pl.BlockSpec((ROW_TILE, wd), lambda i: (i, 0)) for wd in widths],
        out_shape=[jax.ShapeDtypeStruct((t, wd), dt) for wd, dt in zip(widths, dtypes)],
        compiler_params=_params(1),
        name="in_proj",
    )(x2d, w)


def _prep_in_proj_weight(w):
    d = w.shape[0]
    o_q = NSA_WIDTH
    o_kv = o_q + KV_WIDTH
    o_g = o_kv + GATE_WIDTH
    wq = w[:, :o_q] * NSA_HEAD_DIM ** -0.5
    wkv = w[:, o_q:o_kv].reshape(d, N_BRANCH, 2, NSA_KV_HEADS, NSA_HEAD_DIM)
    wkv = wkv.transpose(0, 1, 3, 2, 4)
    wkv0 = wkv[:, 0].reshape(d, KV0_COLS)
    wkv12 = wkv[:, 1:].reshape(d, KV12_COLS)
    wg = jnp.pad(w[:, o_kv:o_g], ((0, 0), (0, GATE_COLS - GATE_WIDTH)))
    wu = w[:, o_g:]
    return jnp.concatenate([wq, wkv0, wkv12, wg, wu], axis=1).astype(BF16)


def _cmp_kv_kernel(x_ref, pe_ref, w1_ref, w2_ref, o_ref):
    n_chunks = x_ref.shape[1] // CMP_STRIDE
    lane = lax.broadcasted_iota(jnp.int32, (1, LANES), 1)
    half = [None, None]
    for l in range(CMP_STRIDE):
        rows = x_ref[0, pl.ds(l, n_chunks, stride=CMP_STRIDE), :]
        for h in range(2):
            part = _dot((rows + pe_ref[h, l:l + 1, :]).astype(BF16), w1_ref[h, l])
            half[h] = part if half[h] is None else half[h] + part
    hid = _gelu(half[0] + pltpu.roll(half[1], n_chunks - 1, 0))
    for kv in range(2):
        part = _dot(hid[:, kv * CMP_HIDDEN:(kv + 1) * CMP_HIDDEN].astype(BF16), w2_ref[kv])
        if kv == 1:
            part = jnp.where(lane >= NSA_HEAD_DIM, 1.0, part)
        o_ref[0, 0, kv] = part.astype(o_ref.dtype)


def _cmp_kv(kv0, pe, w1, w2, batch, seq):
    n_chunks = seq // CMP_STRIDE
    dh = NSA_HEAD_DIM
    x = kv0.reshape(batch, seq, KV0_COLS)
    pe_t = pe.reshape(2, 2, CMP_STRIDE, dh).transpose(1, 2, 0, 3).reshape(2, CMP_STRIDE, LANES)
    w1r = w1.reshape(2, 2, CMP_STRIDE, dh, CMP_HIDDEN)
    zero = jnp.zeros_like(w1r[0])
    w1_t = jnp.concatenate([jnp.concatenate([w1r[0], zero], axis=-1),
                            jnp.concatenate([zero, w1r[1]], axis=-1)], axis=-2).astype(BF16)
    w2p = jnp.pad(w2, ((0, 0), (0, 0), (0, LANES - dh))).astype(BF16)
    return pl.pallas_call(
        _cmp_kv_kernel,
        grid=(batch, NSA_KV_HEADS),
        in_specs=[
            pl.BlockSpec((1, seq, LANES), lambda b, g: (b, 0, g)),
            pl.BlockSpec((2, CMP_STRIDE, LANES), lambda b, g: (0, 0, 0)),
            pl.BlockSpec((2, CMP_STRIDE, LANES, 2 * CMP_HIDDEN), lambda b, g: (0, 0, 0, 0)),
            pl.BlockSpec((2, CMP_HIDDEN, LANES), lambda b, g: (0, 0, 0)),
        ],
        out_specs=pl.BlockSpec((1, 1, 2, n_chunks, LANES), lambda b, g: (b, g, 0, 0, 0)),
        out_shape=jax.ShapeDtypeStruct((batch, NSA_KV_HEADS, 2, n_chunks, LANES), BF16),
        compiler_params=_params(2),
        name="cmp_kv",
    )(x, pe_t, w1_t, w2p)


def _split3(x):
    a = x.astype(BF16)
    r = x - a.astype(F32)
    b = r.astype(BF16)
    c = (r - b.astype(F32)).astype(BF16)
    return a, b, c


def _lane_tiles(s):
    return [s[:, i * LANES:(i + 1) * LANES] for i in range(s.shape[1] // LANES)]


def _row_max_bcast(tiles):
    mt = tiles[0]
    for t in tiles[1:]:
        mt = jnp.maximum(mt, t)
    return jnp.broadcast_to(jnp.max(mt, axis=-1, keepdims=True), mt.shape)


def _nsa_position_masks(t0, n_cmp_rows, seq):
    qt = NSA_Q_TILE
    win_keys = WINDOW + qt
    tq = t0 + lax.broadcasted_iota(jnp.int32, (qt, 1), 0)
    n_idx = lax.broadcasted_iota(jnp.int32, (1, n_cmp_rows), 1)
    n_cmp = (seq - CMP_LEN) // CMP_STRIDE + 1
    valid = (n_idx * CMP_STRIDE + (CMP_LEN - 1) <= tq) & (n_idx < n_cmp)
    wpos = jnp.maximum(t0 - WINDOW, 0) + lax.broadcasted_iota(jnp.int32, (1, win_keys), 1)
    okw = (wpos <= tq) & (wpos > tq - WINDOW)
    return jnp.where(valid, 0.0, NEG_INF), jnp.where(okw, 0.0, NEG_INF)


def _nsa_group(g, t0, masks, q_ref, cmp_ref, kvs_ref, gate_ref, ovl_ref, exp_ref, o_ref,
               sc_scr, gate_scr, m_scr, acc_scr, out_scr, win_scr, seq):
    cmp_bias, win_bias = masks
    qt = NSA_Q_TILE
    rr = NSA_GROUP
    dh = NSA_HEAD_DIM
    m_rows = rr * qt
    n_cmp_rows = cmp_ref.shape[3]
    n_sel = seq // SEL_BLOCK
    win_keys = WINDOW + qt

    def slab(k):
        c0 = (k * NSA_KV_HEADS + g) * LANES
        return kvs_ref.at[:, :, c0:c0 + LANES]

    ksel_ref, vsel_ref, kwin_ref, vwin_ref = (slab(k) for k in (SLAB_KSEL, SLAB_VSEL, SLAB_KWIN, SLAB_VWIN))
    q_ref = q_ref.at[:, :, g * rr * LANES:(g + 1) * rr * LANES]
    cmp_ref = cmp_ref.at[:, g:g + 1]
    o_ref = o_ref.at[:, :, g * rr * dh:(g + 1) * rr * dh]
    sc_scr, gate_scr, m_scr, acc_scr, out_scr, win_scr = (
        r.at[g] for r in (sc_scr, gate_scr, m_scr, acc_scr, out_scr, win_scr))

    qb = q_ref[0]
    qp = jnp.concatenate([qb[:, r * LANES:(r + 1) * LANES] for r in range(rr)], axis=0)
    tq = t0 + lax.broadcasted_iota(jnp.int32, (qt, 1), 0)
    lane = lax.broadcasted_iota(jnp.int32, (1, LANES), 1)
    lower = lane < dh
    gate = jax.nn.sigmoid(gate_ref[0])

    for r in range(rr):
        for branch in range(N_BRANCH):
            c = (g * rr + r) * N_BRANCH + branch
            gate_scr[r * N_BRANCH + branch] = jnp.broadcast_to(gate[:, c:c + 1], (qt, LANES))

    def normalize(acc):
        inv = jnp.where(lower, 1.0 / pltpu.roll(acc, dh, 1), 0.0)
        return acc * inv

    kcmp = cmp_ref[0, 0, 0]
    vcmp = cmp_ref[0, 0, 1]
    s = _dot_nt(qp, kcmp).reshape(rr, qt, n_cmp_rows) + cmp_bias[None]
    e = jnp.exp2(s - jnp.max(s, axis=-1, keepdims=True))
    inv = 1.0 / jnp.sum(e, axis=-1, keepdims=True)
    has_cmp = (tq >= CMP_LEN - 1).astype(F32)
    p = e * (inv * has_cmp[None])
    out_scr[...] = _dot(p.reshape(m_rows, n_cmp_rows).astype(BF16), vcmp)
    yield

    def window_branch():
        w0 = pl.multiple_of(jnp.maximum(t0 - WINDOW, 0), qt)
        kw = kwin_ref[0, pl.ds(w0, win_keys), :]
        vw = vwin_ref[0, pl.ds(w0, win_keys), :]
        s = _dot_nt(qp, kw).reshape(rr, qt, win_keys) + win_bias[None]
        tiles = _lane_tiles(s.reshape(m_rows, win_keys))
        m_b = _row_max_bcast(tiles)
        pw = jnp.concatenate([jnp.exp2(t - m_b) for t in tiles], axis=1).astype(BF16)
        return normalize(_dot(pw, vw))

    psum = p[0]
    for r in range(1, rr):
        psum = psum + p[r]
    ovl_t = ovl_ref[...]
    imp_t = None
    for term in _split3(psum):
        part = _dot_nt(ovl_t, term)
        imp_t = part if imp_t is None else imp_t + part
    imp = imp_t[dh:, :]
    j_s = lax.broadcasted_iota(jnp.int32, (dh, 1), 0)
    tq_l = t0 + lax.broadcasted_iota(jnp.int32, (1, qt), 1)
    cur = tq_l // SEL_BLOCK
    blk_ok = j_s * SEL_BLOCK <= tq_l
    forced = (j_s == 0) | (j_s == cur) | (j_s == cur - 1)
    score = jnp.where(blk_ok, imp + jnp.where(forced, FORCE_BONUS, 0.0), NEG_INF)
    score = jnp.where(j_s < n_sel, score, -jnp.inf)
    sc_scr[...] = score
    yield
    win_scr[...] = window_branch()
    n_grp = dh // SUBLANES
    grp = [score[k * SUBLANES:(k + 1) * SUBLANES] for k in range(n_grp)]
    rank = [jnp.zeros((SUBLANES, qt), F32) for _ in range(n_grp)]
    sub = lax.broadcasted_iota(jnp.int32, (SUBLANES, 1), 0)
    for jp in range(dh):
        row = jnp.broadcast_to(sc_scr[jp:jp + 1, :], (SUBLANES, qt))
        kj = jp // SUBLANES
        for k in range(n_grp):
            if k > kj:
                rank[k] = rank[k] + jnp.where(row >= grp[k], 1.0, 0.0)
            elif k < kj:
                rank[k] = rank[k] + jnp.where(row > grp[k], 1.0, 0.0)
            else:
                ge = jnp.where(row >= grp[k], 1.0, 0.0)
                gt = jnp.where(row > grp[k], 1.0, 0.0)
                rank[k] = rank[k] + jnp.where(sub > jp % SUBLANES, ge, gt)
    n_top = float(min(SEL_TOP, n_sel))
    sel_t = jnp.concatenate(rank, axis=0) < n_top
    sel_t = jnp.where(sel_t & blk_ok, 1.0, 0.0)
    sel = jnp.concatenate([jnp.zeros((dh, qt), F32), sel_t], axis=0).T
    yield

    m_scr[...] = jnp.full(m_scr.shape, NEG_INF, F32)
    acc_scr[...] = jnp.zeros(acc_scr.shape, F32)

    def online_step(tiles, v_slab):
        m_old = m_scr[...]
        m_new = jnp.maximum(m_old, _row_max_bcast(tiles))
        alpha = jnp.exp2(m_old - m_new)
        pexp = jnp.concatenate([jnp.exp2(t - m_new) for t in tiles], axis=1).astype(BF16)
        return m_new, alpha * acc_scr[...] + _dot(pexp, v_slab)

    def diagonal():
        d0 = pl.multiple_of(t0, qt)
        chosen = _dot(sel.astype(BF16), exp_ref[:, pl.ds(d0, qt)])
        kpos = t0 + lax.broadcasted_iota(jnp.int32, (1, qt), 1)
        okd = (chosen > 0.5) & (kpos <= tq)
        kd = ksel_ref[0, pl.ds(d0, qt), :]
        vd = vsel_ref[0, pl.ds(d0, qt), :]
        s = (_dot_nt(qp, kd).reshape(rr, qt, qt) + jnp.where(okd, 0.0, NEG_INF)[None]).reshape(m_rows, qt)
        return normalize(online_step(_lane_tiles(s), vd)[1])

    blk_lane = lane - dh
    bias = jnp.where((sel > 0.5) & (blk_lane * SEL_BLOCK < t0), 0.0, NEG_INF).astype(BF16)
    q_aug = jnp.concatenate([jnp.where(lower, qp[r * qt:(r + 1) * qt], bias) for r in range(rr)], axis=0)
    kc = SEL_KEY_CHUNK

    def step(c):
        k0 = pl.multiple_of(c * kc, kc)
        ks = ksel_ref[0, pl.ds(k0, kc), :]
        vs = vsel_ref[0, pl.ds(k0, kc), :]
        m_scr[...], acc_scr[...] = online_step(_lane_tiles(_dot_nt(q_aug, ks)), vs)

    def finish(o_sel, o_win):
        branches = (out_scr[...], o_sel, o_win)
        heads = []
        for r in range(rr):
            rows = slice(r * qt, (r + 1) * qt)
            comb = gate_scr[r * N_BRANCH] * branches[0][rows]
            for branch in range(1, N_BRANCH):
                comb = comb + gate_scr[r * N_BRANCH + branch] * branches[branch][rows]
            heads.append(comb)
        for pair in range(rr // 2):
            odd = pltpu.roll(heads[2 * pair + 1], dh, 1)
            o_ref[0, :, pair * LANES:(pair + 1) * LANES] = jnp.where(lower, heads[2 * pair], odd).astype(o_ref.dtype)

    return step, lambda: win_scr[...], diagonal, finish


def _nsa_kernel(*refs, seq):
    t0 = pl.program_id(1) * NSA_Q_TILE
    masks = _nsa_position_masks(t0, refs[1].shape[3], seq)
    pending = {g: _nsa_group(g, t0, masks, *refs, seq) for g in range(NSA_KV_HEADS)}
    done = {}
    while pending:
        for g in sorted(pending):
            try:
                next(pending[g])
            except StopIteration as stop:
                done[g] = stop.value
                del pending[g]
    groups = [done[g] for g in sorted(done)]

    def sel_step(c, carry):
        for grp in groups:
            grp[0](c)
        return carry

    lax.fori_loop(0, (t0 + SEL_KEY_CHUNK - 1) // SEL_KEY_CHUNK, sel_step, 0)
    o_win = [grp[1]() for grp in groups]
    o_sel = [grp[2]() for grp in groups]
    for grp, sel_out, win_out in zip(groups, o_sel, o_win):
        grp[3](sel_out, win_out)


def _nsa_attention(qpad, cmp_slabs, kvs, gates, batch, seq):
    qt = NSA_Q_TILE
    dh = NSA_HEAD_DIM
    n_sel = seq // SEL_BLOCK
    n_chunks = seq // CMP_STRIDE
    assert n_sel <= dh and seq % SEL_KEY_CHUNK == 0 and seq >= WINDOW + qt and SEL_KEY_CHUNK % qt == 0
    n_cmp = (seq - CMP_LEN) // CMP_STRIDE + 1
    cmp_start = np.arange(n_chunks) * CMP_STRIDE
    blk = np.arange(LANES) - dh
    overlap_t = ((cmp_start[None, :] < blk[:, None] * SEL_BLOCK + SEL_BLOCK)
                 & (cmp_start[None, :] + CMP_LEN > blk[:, None] * SEL_BLOCK)
                 & (np.arange(n_chunks)[None, :] < n_cmp) & (blk[:, None] >= 0) & (blk[:, None] < n_sel))
    overlap_t = jnp.asarray(overlap_t.astype(np.float32), BF16)
    expand = (np.arange(seq)[None, :] // SEL_BLOCK) == blk[:, None]
    expand = jnp.asarray(expand.astype(np.float32), BF16)
    m_rows = NSA_GROUP * qt
    ng = NSA_KV_HEADS
    return pl.pallas_call(
        functools.partial(_nsa_kernel, seq=seq),
        grid=(batch, seq // qt),
        in_specs=[
            pl.BlockSpec((1, qt, Q_COLS), lambda b, i: (b, i, 0)),
            pl.BlockSpec((1, ng, 2, n_chunks, LANES), lambda b, i: (b, 0, 0, 0, 0)),
            pl.BlockSpec((1, seq, KVS_COLS), lambda b, i: (b, 0, 0)),
            pl.BlockSpec((1, qt, GATE_COLS), lambda b, i: (b, i, 0)),
            pl.BlockSpec((LANES, n_chunks), lambda b, i: (0, 0)),
            pl.BlockSpec((LANES, seq), lambda b, i: (0, 0)),
        ],
        out_specs=pl.BlockSpec((1, qt, NSA_WIDTH), lambda b, i: (b, i, 0)),
        out_shape=jax.ShapeDtypeStruct((batch, seq, NSA_WIDTH), BF16),
        scratch_shapes=[
            pltpu.VMEM((ng, dh, qt), F32),
            pltpu.VMEM((ng, NSA_GROUP * N_BRANCH, qt, LANES), F32),
            pltpu.VMEM((ng, m_rows, LANES), F32),
            pltpu.VMEM((ng, m_rows, LANES), F32),
            pltpu.VMEM((ng, m_rows, LANES), F32),
            pltpu.VMEM((ng, m_rows, LANES), F32),
        ],
        compiler_params=_params(2),
        name="nsa_attention",
    )(qpad, cmp_slabs, kvs, gates, overlap_t, expand)


def _pool_out_rows(rows, x_ref, nsa_ref, u_ref, uh_ref, pw_ref, ps_ref, wo_ref, g_ref, b_ref, o_ref, seq):
    tm = x_ref.shape[0]
    n_rows = rows.stop - rows.start
    i = pl.program_id(0)
    tiles_per_seq = seq // tm
    t_seq = (i % tiles_per_seq) * tm + rows.start + lax.broadcasted_iota(jnp.int32, (n_rows, 1), 0)
    u = u_ref[rows, :]
    if rows.start == 0:
        halo = jnp.where((i % tiles_per_seq) == 0, 0.0, uh_ref[...])
    else:
        halo = u_ref[rows.start - POOL_HALO:rows.start, :]
    z = jnp.concatenate([halo, u], axis=0)
    gd = POOL_GROUP_DIM
    y = jnp.dot(nsa_ref[rows, :], wo_ref[0:NSA_WIDTH, :], preferred_element_type=F32)
    yield
    mixed = []
    for gi, w in enumerate(POOL_WINDOWS):
        run = z[:, gi * gd:(gi + 1) * gd]
        span = 1
        while span < w:
            run = run + pltpu.roll(run, span, 0)
            span *= 2
        cnt = jnp.minimum(t_seq + 1, w).astype(F32)
        pooled = run[POOL_HALO:, :] / cnt - u[:, gi * gd:(gi + 1) * gd]
        mixed.append((_dot(pooled.astype(BF16), pw_ref[gi]) * ps_ref[:, gi * gd:(gi + 1) * gd]).astype(BF16))
    yield
    y = y + _dot(jnp.concatenate(mixed, axis=1), wo_ref[NSA_WIDTH:, :])
    yield
    o_ref[rows, :] = _layer_norm(ALPHA * x_ref[rows, :] + y, g_ref[...], b_ref[...])


def _pool_out_kernel(*refs, seq):
    _run_staggered([_pool_out_rows(rows, *refs, seq) for rows in _row_slices(refs[0].shape[0])])


def _pool_out(x2d, nsa2d, u2d, pool_w, pool_scale, w_out, ln_g, ln_b, seq):
    t = x2d.shape[0]
    tm = SPLIT_ROW_TILE
    hb = tm // POOL_HALO
    return pl.pallas_call(
        functools.partial(_pool_out_kernel, seq=seq),
        grid=(t // tm,),
        in_specs=[
            pl.BlockSpec((tm, D_MODEL), lambda i: (i, 0)),
            pl.BlockSpec((tm, NSA_WIDTH), lambda i: (i, 0)),
            pl.BlockSpec((tm, POOL_WIDTH), lambda i: (i, 0)),
            pl.BlockSpec((POOL_HALO, POOL_WIDTH), lambda i: (jnp.maximum(i * hb - 1, 0), 0)),
            pl.BlockSpec(pool_w.shape, lambda i: (0, 0, 0)),
            pl.BlockSpec((1, POOL_WIDTH), lambda i: (0, 0)),
            pl.BlockSpec(w_out.shape, lambda i: (0, 0)),
            pl.BlockSpec((1, D_MODEL), lambda i: (0, 0)),
            pl.BlockSpec((1, D_MODEL), lambda i: (0, 0)),
        ],
        out_specs=pl.BlockSpec((tm, D_MODEL), lambda i: (i, 0)),
        out_shape=jax.ShapeDtypeStruct((t, D_MODEL), F32),
        compiler_params=_params(1),
        name="pool_out_ln",
    )(x2d, nsa2d, u2d, u2d, pool_w.astype(BF16), pool_scale.reshape(1, -1), w_out.astype(BF16),
      ln_g.reshape(1, -1), ln_b.reshape(1, -1))


def _sgu_rows(rows, x_ref, win_ref, ng_ref, nb_ref, ws_ref, bs_ref, wo_ref, g_ref, b_ref, o_ref, sv_scr):
    n_rows = rows.stop - rows.start
    n_chunk = n_rows // SG_CHUNK
    gd = SG_GROUP_DIM
    x = x_ref[rows, :]
    xb = x.astype(BF16)
    hu = _dot(xb, win_ref[:, 0:SG_WIDTH])
    hv = _dot(xb, win_ref[:, SG_WIDTH:2 * SG_WIDTH])
    yield
    zu = _gelu(hu)
    v = _layer_norm(_gelu(hv), ng_ref[...], nb_ref[...]).astype(BF16)
    yield
    row = lax.broadcasted_iota(jnp.int32, (SG_CHUNK, SG_CHUNK), 0)
    col = lax.broadcasted_iota(jnp.int32, (SG_CHUNK, SG_CHUNK), 1)
    causal = col <= row
    for h in range(SG_GROUPS):
        w = jnp.where(causal, ws_ref[h], 0.0).astype(BF16)
        rhs = jnp.concatenate(
            [v[c * SG_CHUNK:(c + 1) * SG_CHUNK, h * gd:(h + 1) * gd] for c in range(n_chunk)], axis=1)
        res = _dot(w, rhs)
        for c in range(n_chunk):
            r0 = rows.start + c * SG_CHUNK
            sv_scr[r0:r0 + SG_CHUNK, h * gd:(h + 1) * gd] = res[:, c * gd:(c + 1) * gd]
    bias = bs_ref[...]
    gated = jnp.concatenate(
        [zu[c * SG_CHUNK:(c + 1) * SG_CHUNK, :]
         * (sv_scr[rows.start + c * SG_CHUNK:rows.start + (c + 1) * SG_CHUNK, :] + bias) for c in range(n_chunk)],
        axis=0).astype(BF16)
    yield
    y = _dot(gated, wo_ref[...])
    yield
    o_ref[rows, :] = _layer_norm(ALPHA * x + y, g_ref[...], b_ref[...])


def _sgu_kernel(*refs):
    _run_staggered([_sgu_rows(rows, *refs) for rows in _row_slices(refs[0].shape[0])])


def _sgu(x2d, w_in, norm_g, norm_b, w_s, b_s, w_out, ln_g, ln_b):
    t = x2d.shape[0]
    tm = SPLIT_ROW_TILE
    bias = jnp.repeat(b_s.T, SG_GROUP_DIM, axis=1)
    vec = lambda a: a.reshape(1, -1)
    const2 = lambda i: (0, 0)
    return pl.pallas_call(
        _sgu_kernel,
        grid=(t // tm,),
        in_specs=[
            pl.BlockSpec((tm, D_MODEL), lambda i: (i, 0)),
            pl.BlockSpec(w_in.shape, const2),
            pl.BlockSpec((1, SG_WIDTH), const2),
            pl.BlockSpec((1, SG_WIDTH), const2),
            pl.BlockSpec(w_s.shape, lambda i: (0, 0, 0)),
            pl.BlockSpec(bias.shape, const2),
            pl.BlockSpec(w_out.shape, const2),
            pl.BlockSpec((1, D_MODEL), const2),
            pl.BlockSpec((1, D_MODEL), const2),
        ],
        out_specs=pl.BlockSpec((tm, D_MODEL), lambda i: (i, 0)),
        out_shape=jax.ShapeDtypeStruct((t, D_MODEL), F32),
        scratch_shapes=[pltpu.VMEM((tm, SG_WIDTH), F32)],
        compiler_params=_params(1),
        name="sgu_out_ln",
    )(x2d, w_in.astype(BF16), vec(norm_g), vec(norm_b), w_s, bias, w_out.astype(BF16), vec(ln_g), vec(ln_b))


def _mem_kv_kernel(m_ref, w_ref, o_ref):
    o_ref[...] = _dot(m_ref[...].astype(BF16), w_ref[...]).astype(o_ref.dtype)


def _mem_kv(mem2d, wkv):
    rows = mem2d.shape[0]
    n = wkv.shape[1]
    tm = min(ROW_TILE, rows)
    return pl.pallas_call(
        _mem_kv_kernel,
        grid=(rows // tm,),
        in_specs=[pl.BlockSpec((tm, D_MODEL), lambda i: (i, 0)), pl.BlockSpec(wkv.shape, lambda i: (0, 0))],
        out_specs=pl.BlockSpec((tm, n), lambda i: (i, 0)),
        out_shape=jax.ShapeDtypeStruct((rows, n), BF16),
        compiler_params=_params(1),
        name="mem_kv",
    )(mem2d, wkv.astype(BF16))


def _mem_attn_rows(rows, x_ref, kv_ref, wq_ref, wo_ref, g_ref, b_ref, o_ref):
    x = x_ref[rows, :]
    q = _dot(x.astype(BF16), wq_ref[...]).astype(BF16)
    yield
    hd = MEM_HEAD_DIM
    width = MEM_HEADS * hd
    outs = []
    for h in range(MEM_HEADS):
        k = kv_ref[0, :, h * hd:(h + 1) * hd]
        v = kv_ref[0, :, width + h * hd:width + (h + 1) * hd]
        s = _dot_nt(q[:, h * hd:(h + 1) * hd], k)
        e = jnp.exp(s - jnp.max(s, axis=-1, keepdims=True))
        p = e * (1.0 / jnp.sum(e, axis=-1, keepdims=True))
        outs.append(_dot(p.astype(BF16), v).astype(BF16))
    yield
    y = _dot(jnp.concatenate(outs, axis=1), wo_ref[...])
    yield
    o_ref[rows, :] = _layer_norm(ALPHA * x + y, g_ref[...], b_ref[...])


def _mem_attn_kernel(*refs):
    _run_staggered([_mem_attn_rows(rows, *refs) for rows in _row_slices(refs[0].shape[0])])


def _mem_attn(x2d, memkv, wq, wo, ln_g, ln_b, seq):
    t = x2d.shape[0]
    tm = SPLIT_ROW_TILE
    tiles_per_seq = seq // tm
    mem_len, kv_cols = memkv.shape[1], memkv.shape[2]
    const2 = lambda i: (0, 0)
    wq_s = (wq * MEM_HEAD_DIM ** -0.5).astype(BF16)
    return pl.pallas_call(
        _mem_attn_kernel,
        grid=(t // tm,),
        in_specs=[
            pl.BlockSpec((tm, D_MODEL), lambda i: (i, 0)),
            pl.BlockSpec((1, mem_len, kv_cols), lambda i: (i // tiles_per_seq, 0, 0)),
            pl.BlockSpec(wq.shape, const2),
            pl.BlockSpec(wo.shape, const2),
            pl.BlockSpec((1, D_MODEL), const2),
            pl.BlockSpec((1, D_MODEL), const2),
        ],
        out_specs=pl.BlockSpec((tm, D_MODEL), lambda i: (i, 0)),
        out_shape=jax.ShapeDtypeStruct((t, D_MODEL), F32),
        compiler_params=_params(1),
        name="mem_attn_ln",
    )(x2d, memkv, wq_s, wo.astype(BF16), ln_g.reshape(1, -1), ln_b.reshape(1, -1))


def _ffn_kernel(x_ref, xh_ref, wup_ref, cw_ref, cb_ref, wd_ref, g_ref, b_ref, o_ref, *, seq):
    tm = x_ref.shape[0]
    i = pl.program_id(0)
    first = (i % (seq // tm)) == 0
    x = x_ref[...]
    halo = jnp.where(first, 0.0, xh_ref[...])
    xcat = jnp.concatenate([halo, x], axis=0).astype(BF16)
    n_steps = D_FF // FFN_CHUNK

    def cols(ref, c):
        lo = c * FFN_CHUNK
        return jnp.concatenate([ref[:, lo:lo + FFN_CHUNK], ref[:, D_FF + lo:D_FF + lo + FFN_CHUNK]], axis=1)

    def up(c):
        lo = c * FFN_CHUNK
        return jnp.concatenate([_dot(xcat, wup_ref[:, lo:lo + FFN_CHUNK]),
                                _dot(xcat, wup_ref[:, D_FF + lo:D_FF + lo + FFN_CHUNK])], axis=1)

    def shift_rows(v):
        blocks = v.reshape(v.shape[0] // SUBLANES, SUBLANES, v.shape[1])
        rot = pltpu.roll(blocks, 1, 1)
        prev = jnp.concatenate([rot[-1:], rot[:-1]], axis=0)
        sub = lax.broadcasted_iota(jnp.int32, (1, SUBLANES, 1), 1)
        return jnp.where(sub == 0, prev, rot).reshape(v.shape)

    y = None
    acts = []
    h_next = up(0)
    for c in range(n_steps):
        h = h_next
        if c + 1 < n_steps:
            h_next = up(c + 1)
        cw = cols(cw_ref, c)
        out = cw[0:1, :] * h
        for j in range(1, CONV_WIDTH):
            out = cw[j:j + 1, :] * h + shift_rows(out)
        out = out[CONV_HALO:, :] + cols(cb_ref, c)
        a, gt = out[:, :FFN_CHUNK], out[:, FFN_CHUNK:]
        acts.append((a * (gt * jax.nn.sigmoid(gt))).astype(BF16))
        if len(acts) == FFN_DOWN_GROUP or c + 1 == n_steps:
            r0 = (c + 1 - len(acts)) * FFN_CHUNK
            part = _dot(jnp.concatenate(acts, axis=1), wd_ref[r0:(c + 1) * FFN_CHUNK, :])
            y = part if y is None else y + part
            acts = []
    o_ref[...] = _layer_norm(ALPHA * x + y, g_ref[...], b_ref[...])


def _ffn(x2d, w_up, conv_w, conv_b, w_down, ln_g, ln_b, seq):
    t = x2d.shape[0]
    tm = FFN_ROW_TILE
    hb = tm // CONV_HALO
    const2 = lambda i: (0, 0)
    return pl.pallas_call(
        functools.partial(_ffn_kernel, seq=seq),
        grid=(t // tm,),
        in_specs=[
            pl.BlockSpec((tm, D_MODEL), lambda i: (i, 0)),
            pl.BlockSpec((CONV_HALO, D_MODEL), lambda i: (jnp.maximum(i * hb - 1, 0), 0)),
            pl.BlockSpec(w_up.shape, const2, pipeline_mode=pl.Buffered(1)),
            pl.BlockSpec(conv_w.shape, const2),
            pl.BlockSpec((1, 2 * D_FF), const2),
            pl.BlockSpec(w_down.shape, const2, pipeline_mode=pl.Buffered(1)),
            pl.BlockSpec((1, D_MODEL), const2),
            pl.BlockSpec((1, D_MODEL), const2),
        ],
        out_specs=pl.BlockSpec((tm, D_MODEL), lambda i: (i, 0)),
        out_shape=jax.ShapeDtypeStruct((t, D_MODEL), F32),
        compiler_params=_params(1),
        name="conv_ffn_ln",
    )(x2d, x2d, w_up.astype(BF16), conv_w, conv_b.reshape(1, -1), w_down.astype(BF16),
      ln_g.reshape(1, -1), ln_b.reshape(1, -1))


def kernel(x, mem, ab_w_in, nsa_cmp_pe, nsa_cmp_w1, nsa_cmp_w2, pool_w, pool_scale, ab_w_out,
           sg_w_in, sg_norm_g, sg_norm_b, sg_w_s, sg_b_s, sg_w_out, ln_g, ln_b,
           mem_wq, mem_wkv, mem_wo, ffn_w_up, ffn_conv_w, ffn_conv_b, ffn_w_down):
    batch, seq, d = x.shape
    assert d == D_MODEL and seq % NSA_Q_TILE == 0
    assert all(seq % tile == 0 for tile in (ROW_TILE, SPLIT_ROW_TILE, FFN_ROW_TILE))
    t = batch * seq
    h = x.reshape(t, d)
    mem2d = mem.reshape(batch * mem.shape[1], d)
    for layer in range(DEPTH):
        i = layer // 2
        if layer % 2 == 0:
            qpad, kv0, kvs, gates, u = _in_proj(h, _prep_in_proj_weight(ab_w_in[i]), seq)
            kvc = _cmp_kv(kv0, nsa_cmp_pe[i], nsa_cmp_w1[i], nsa_cmp_w2[i], batch, seq)
            nsa = _nsa_attention(qpad.reshape(batch, seq, Q_COLS), kvc, kvs.reshape(batch, seq, KVS_COLS),
                                 gates.reshape(batch, seq, GATE_COLS), batch, seq)
            h = _pool_out(h, nsa.reshape(t, NSA_WIDTH), u, pool_w[i], pool_scale[i], ab_w_out[i],
                          ln_g[layer, 0], ln_b[layer, 0], seq)
        else:
            h = _sgu(h, sg_w_in[i], sg_norm_g[i], sg_norm_b[i], sg_w_s[i], sg_b_s[i], sg_w_out[i],
                     ln_g[layer, 0], ln_b[layer, 0])
        memkv = _mem_kv(mem2d, mem_wkv[layer]).reshape(batch, mem.shape[1], -1)
        h = _mem_attn(h, memkv, mem_wq[layer], mem_wo[layer], ln_g[layer, 1], ln_b[layer, 1], seq)
        h = _ffn(h, ffn_w_up[layer], ffn_conv_w[layer], ffn_conv_b[layer], ffn_w_down[layer],
                 ln_g[layer, 2], ln_b[layer, 2], seq)
    return h.reshape(batch, seq, d)
```

```python
import functools

import numpy as np
import jax
import jax.numpy as jnp
from jax import lax
from jax.experimental import pallas as pl
from jax.experimental.pallas import tpu as pltpu

F32 = jnp.float32
BF16 = jnp.bfloat16

LANES = 128
SUBLANES = 8
VMEM_LIMIT_BYTES = 56 * 1024 * 1024

D_MODEL = 1024
DEPTH = 2
NSA_HEADS = 8
NSA_KV_HEADS = 2
NSA_GROUP = NSA_HEADS // NSA_KV_HEADS
NSA_HEAD_DIM = 64
NSA_WIDTH = NSA_HEADS * NSA_HEAD_DIM
N_BRANCH = 3
CMP_LEN = 32
CMP_STRIDE = 16
CMP_HIDDEN = 2 * NSA_HEAD_DIM
SEL_BLOCK = 64
SEL_TOP = 16
WINDOW = 512
FORCE_BONUS = 100.0
KV_WIDTH = N_BRANCH * 2 * NSA_KV_HEADS * NSA_HEAD_DIM
GATE_WIDTH = NSA_HEADS * N_BRANCH
POOL_WINDOWS = (2, 4, 8, 16)
POOL_WIDTH = D_MODEL // 2
POOL_GROUP_DIM = POOL_WIDTH // len(POOL_WINDOWS)
SG_CHUNK = 128
SG_GROUPS = 8
SG_WIDTH = D_MODEL
SG_GROUP_DIM = SG_WIDTH // SG_GROUPS
MEM_HEADS = 4
MEM_HEAD_DIM = D_MODEL // MEM_HEADS
D_FF = 256 * ((8 * D_MODEL // 3 + 255) // 256)
CONV_WIDTH = 3
ALPHA = (2 * DEPTH) ** 0.25
LN_EPS = 1e-5
NEG_INF = -1e30

ROW_TILE = 512
STREAM_ROWS = 512
SPLIT_ROW_TILE = 1024
MEM_ROW_TILE = 2048
FFN_ROW_TILE = 1024
NSA_Q_TILE = 256
SEL_KEY_CHUNK = 512
FFN_CHUNK = 256
FFN_DOWN_GROUP = 11
POOL_HALO = 16
CONV_HALO = SUBLANES


def _params(n_parallel):
    return pltpu.CompilerParams(
        dimension_semantics=("parallel",) * n_parallel,
        vmem_limit_bytes=VMEM_LIMIT_BYTES,
    )


def _layer_norm(r, g, b):
    mu = jnp.mean(r, axis=-1, keepdims=True)
    d = r - mu
    var = jnp.mean(d * d, axis=-1, keepdims=True)
    return d * lax.rsqrt(var + LN_EPS) * g + b


def _gelu(x):
    c = np.sqrt(2.0 / np.pi).astype(np.float32)
    return 0.5 * x * (1.0 + jnp.tanh(c * (x + 0.044715 * (x * x * x))))


def _dot(a, b):
    return jnp.dot(a, b, preferred_element_type=F32)


def _dot_nt(a, b):
    return lax.dot_general(a, b, (((1,), (1,)), ((), ())), preferred_element_type=F32)


def _row_slices(tm):
    return [slice(r0, r0 + STREAM_ROWS) for r0 in range(0, tm, STREAM_ROWS)]


def _run_staggered(streams):
    live = dict(enumerate(streams))
    tick = 0
    while live:
        for k in sorted(live):
            if tick >= k:
                try:
                    next(live[k])
                except StopIteration:
                    del live[k]
        tick += 1


LOG2E = float(np.log2(np.e))
Q_COLS = NSA_HEADS * LANES
KV0_COLS = NSA_KV_HEADS * LANES
SLAB_KSEL, SLAB_VSEL, SLAB_KWIN, SLAB_VWIN = range(4)
N_SLAB = 4
KVS_COLS = N_SLAB * NSA_KV_HEADS * LANES
GATE_COLS = LANES
KV12_COLS = 2 * NSA_KV_HEADS * LANES
IN_PROJ_SPLITS = [int(v) for v in np.cumsum([0, NSA_WIDTH, KV0_COLS, KV12_COLS, GATE_COLS, POOL_WIDTH])]


def _in_proj_rows(rows, x_ref, w_ref, q_ref, kv0_ref, kvs_ref, gate_ref, u_ref, seq):
    tm = x_ref.shape[0]
    n_rows = rows.stop - rows.start
    ng = NSA_KV_HEADS
    xb = x_ref[rows, :].astype(BF16)
    q, kv0, kv, gates, u = (_dot(xb, w_ref[:, IN_PROJ_SPLITS[k]:IN_PROJ_SPLITS[k + 1]]) for k in range(5))
    yield

    def tile(a, k):
        return a[:, k * LANES:(k + 1) * LANES]

    def put(ref, k, lower_part, fill):
        ref[rows, k * LANES:(k + 1) * LANES] = jnp.where(lower, lower_part, fill).astype(ref.dtype)

    pos = ((pl.program_id(0) % (seq // tm)) * tm + rows.start
           + lax.broadcasted_iota(jnp.int32, (n_rows, 1), 0))
    lane = lax.broadcasted_iota(jnp.int32, (1, LANES), 1)
    lower = lane < NSA_HEAD_DIM
    onehot = jnp.where(pos // SEL_BLOCK == lane - NSA_HEAD_DIM, 1.0, 0.0)

    q = q * LOG2E
    for pair in range(NSA_HEADS // 2):
        both = tile(q, pair)
        put(q_ref, 2 * pair, both, 0.0)
        put(q_ref, 2 * pair + 1, pltpu.roll(both, NSA_HEAD_DIM, 1), 0.0)
    kv0_ref[rows, :] = kv0
    gate_ref[rows, :] = gates
    u_ref[rows, :] = u
    for branch, (kslab, vslab, kfill) in enumerate(((SLAB_KSEL, SLAB_VSEL, onehot), (SLAB_KWIN, SLAB_VWIN, 0.0))):
        for g in range(ng):
            both = tile(kv, branch * ng + g)
            put(kvs_ref, kslab * ng + g, both, kfill)
            put(kvs_ref, vslab * ng + g, pltpu.roll(both, NSA_HEAD_DIM, 1), 1.0)


def _in_proj_kernel(*refs, seq):
    _run_staggered([_in_proj_rows(rows, *refs, seq) for rows in _row_slices(refs[0].shape[0])])


def _in_proj(x2d, w, seq):
    t = x2d.shape[0]
    n = w.shape[1]
    widths = [Q_COLS, KV0_COLS, KVS_COLS, GATE_COLS, POOL_WIDTH]
    dtypes = [BF16, F32, BF16, F32, F32]
    return pl.pallas_call(
        functools.partial(_in_proj_kernel, seq=seq),
        grid=(t // SPLIT_ROW_TILE,),
        in_specs=[
            pl.BlockSpec((SPLIT_ROW_TILE, D_MODEL), lambda i: (i, 0)),
            pl.BlockSpec((D_MODEL, n), lambda i: (0, 0)),
        ],
        out_specs=[pl.BlockSpec((SPLIT_ROW_TILE, wd), lambda i: (i, 0)) for wd in widths],
        out_shape=[jax.ShapeDtypeStruct((t, wd), dt) for wd, dt in zip(widths, dtypes)],
        compiler_params=_params(1),
        name="in_proj",
    )(x2d, w)


def _prep_in_proj_weight(w):
    d = w.shape[0]
    o_q = NSA_WIDTH
    o_kv = o_q + KV_WIDTH
    o_g = o_kv + GATE_WIDTH
    wq = w[:, :o_q] * NSA_HEAD_DIM ** -0.5
    wkv = w[:, o_q:o_kv].reshape(d, N_BRANCH, 2, NSA_KV_HEADS, NSA_HEAD_DIM)
    wkv = wkv.transpose(0, 1, 3, 2, 4)
    wkv0 = wkv[:, 0].reshape(d, KV0_COLS)
    wkv12 = wkv[:, 1:].reshape(d, KV12_COLS)
    wg = jnp.pad(w[:, o_kv:o_g], ((0, 0), (0, GATE_COLS - GATE_WIDTH)))
    wu = w[:, o_g:]
    return jnp.concatenate([wq, wkv0, wkv12, wg, wu], axis=1).astype(BF16)


def _cmp_kv_kernel(x_ref, pe_ref, w1_ref, w2_ref, o_ref):
    n_chunks = x_ref.shape[1] // CMP_STRIDE
    lane = lax.broadcasted_iota(jnp.int32, (1, LANES), 1)
    half = [None, None]
    for l in range(CMP_STRIDE):
        rows = x_ref[0, pl.ds(l, n_chunks, stride=CMP_STRIDE), :]
        for h in range(2):
            part = _dot((rows + pe_ref[h, l:l + 1, :]).astype(BF16), w1_ref[h, l])
            half[h] = part if half[h] is None else half[h] + part
    hid = _gelu(half[0] + pltpu.roll(half[1], n_chunks - 1, 0))
    for kv in range(2):
        part = _dot(hid[:, kv * CMP_HIDDEN:(kv + 1) * CMP_HIDDEN].astype(BF16), w2_ref[kv])
        if kv == 1:
            part = jnp.where(lane >= NSA_HEAD_DIM, 1.0, part)
        o_ref[0, 0, kv] = part.astype(o_ref.dtype)


def _cmp_kv(kv0, pe, w1, w2, batch, seq):
    n_chunks = seq // CMP_STRIDE
    dh = NSA_HEAD_DIM
    x = kv0.reshape(batch, seq, KV0_COLS)
    pe_t = pe.reshape(2, 2, CMP_STRIDE, dh).transpose(1, 2, 0, 3).reshape(2, CMP_STRIDE, LANES)
    w1r = w1.reshape(2, 2, CMP_STRIDE, dh, CMP_HIDDEN)
    zero = jnp.zeros_like(w1r[0])
    w1_t = jnp.concatenate([jnp.concatenate([w1r[0], zero], axis=-1),
                            jnp.concatenate([zero, w1r[1]], axis=-1)], axis=-2).astype(BF16)
    w2p = jnp.pad(w2, ((0, 0), (0, 0), (0, LANES - dh))).astype(BF16)
    return pl.pallas_call(
        _cmp_kv_kernel,
        grid=(batch, NSA_KV_HEADS),
        in_specs=[
            pl.BlockSpec((1, seq, LANES), lambda b, g: (b, 0, g)),
            pl.BlockSpec((2, CMP_STRIDE, LANES), lambda b, g: (0, 0, 0)),
            pl.BlockSpec((2, CMP_STRIDE, LANES, 2 * CMP_HIDDEN), lambda b, g: (0, 0, 0, 0)),
            pl.BlockSpec((2, CMP_HIDDEN, LANES), lambda b, g: (0, 0, 0)),
        ],
        out_specs=pl.BlockSpec((1, 1, 2, n_chunks, LANES), lambda b, g: (b, g, 0, 0, 0)),
        out_shape=jax.ShapeDtypeStruct((batch, NSA_KV_HEADS, 2, n_chunks, LANES), BF16),
        compiler_params=_params(2),
        name="cmp_kv",
    )(x, pe_t, w1_t, w2p)


def _split3(x):
    a = x.astype(BF16)
    r = x - a.astype(F32)
    b = r.astype(BF16)
    c = (r - b.astype(F32)).astype(BF16)
    return a, b, c


def _lane_tiles(s):
    return [s[:, i * LANES:(i + 1) * LANES] for i in range(s.shape[1] // LANES)]


def _row_max_bcast(tiles):
    mt = tiles[0]
    for t in tiles[1:]:
        mt = jnp.maximum(mt, t)
    return jnp.broadcast_to(jnp.max(mt, axis=-1, keepdims=True), mt.shape)


def _nsa_position_masks(t0, n_cmp_rows, seq):
    qt = NSA_Q_TILE
    win_keys = WINDOW + qt
    tq = t0 + lax.broadcasted_iota(jnp.int32, (qt, 1), 0)
    n_idx = lax.broadcasted_iota(jnp.int32, (1, n_cmp_rows), 1)
    n_cmp = (seq - CMP_LEN) // CMP_STRIDE + 1
    valid = (n_idx * CMP_STRIDE + (CMP_LEN - 1) <= tq) & (n_idx < n_cmp)
    wpos = jnp.maximum(t0 - WINDOW, 0) + lax.broadcasted_iota(jnp.int32, (1, win_keys), 1)
    okw = (wpos <= tq) & (wpos > tq - WINDOW)
    return jnp.where(valid, 0.0, NEG_INF), jnp.where(okw, 0.0, NEG_INF)


def _nsa_group(g, t0, masks, q_ref, cmp_ref, kvs_ref, gate_ref, ovl_ref, exp_ref, o_ref,
               sc_scr, gate_scr, m_scr, acc_scr, out_scr, win_scr, seq):
    cmp_bias, win_bias = masks
    qt = NSA_Q_TILE
    rr = NSA_GROUP
    dh = NSA_HEAD_DIM
    m_rows = rr * qt
    n_cmp_rows = cmp_ref.shape[3]
    n_sel = seq // SEL_BLOCK
    win_keys = WINDOW + qt

    def slab(k):
        c0 = (k * NSA_KV_HEADS + g) * LANES
        return kvs_ref.at[:, :, c0:c0 + LANES]

    ksel_ref, vsel_ref, kwin_ref, vwin_ref = (slab(k) for k in (SLAB_KSEL, SLAB_VSEL, SLAB_KWIN, SLAB_VWIN))
    q_ref = q_ref.at[:, :, g * rr * LANES:(g + 1) * rr * LANES]
    cmp_ref = cmp_ref.at[:, g:g + 1]
    o_ref = o_ref.at[:, :, g * rr * dh:(g + 1) * rr * dh]
    sc_scr, gate_scr, m_scr, acc_scr, out_scr, win_scr = (
        r.at[g] for r in (sc_scr, gate_scr, m_scr, acc_scr, out_scr, win_scr))

    qb = q_ref[0]
    qp = jnp.concatenate([qb[:, r * LANES:(r + 1) * LANES] for r in range(rr)], axis=0)
    tq = t0 + lax.broadcasted_iota(jnp.int32, (qt, 1), 0)
    lane = lax.broadcasted_iota(jnp.int32, (1, LANES), 1)
    lower = lane < dh
    gate = jax.nn.sigmoid(gate_ref[0])

    for r in range(rr):
        for branch in range(N_BRANCH):
            c = (g * rr + r) * N_BRANCH + branch
            gate_scr[r * N_BRANCH + branch] = jnp.broadcast_to(gate[:, c:c + 1], (qt, LANES))

    def normalize(acc):
        inv = jnp.where(lower, 1.0 / pltpu.roll(acc, dh, 1), 0.0)
        return acc * inv

    kcmp = cmp_ref[0, 0, 0]
    vcmp = cmp_ref[0, 0, 1]
    s = _dot_nt(qp, kcmp).reshape(rr, qt, n_cmp_rows) + cmp_bias[None]
    e = jnp.exp2(s - jnp.max(s, axis=-1, keepdims=True))
    inv = 1.0 / jnp.sum(e, axis=-1, keepdims=True)
    has_cmp = (tq >= CMP_LEN - 1).astype(F32)
    p = e * (inv * has_cmp[None])
    out_scr[...] = _dot(p.reshape(m_rows, n_cmp_rows).astype(BF16), vcmp)
    yield

    def window_branch():
        w0 = pl.multiple_of(jnp.maximum(t0 - WINDOW, 0), qt)
        kw = kwin_ref[0, pl.ds(w0, win_keys), :]
        vw = vwin_ref[0, pl.ds(w0, win_keys), :]
        s = _dot_nt(qp, kw).reshape(rr, qt, win_keys) + win_bias[None]
        tiles = _lane_tiles(s.reshape(m_rows, win_keys))
        m_b = _row_max_bcast(tiles)
        pw = jnp.concatenate([jnp.exp2(t - m_b) for t in tiles], axis=1).astype(BF16)
        return normalize(_dot(pw, vw))

    psum = p[0]
    for r in range(1, rr):
        psum = psum + p[r]
    ovl_t = ovl_ref[...]
    imp_t = None
    for term in _split3(psum):
        part = _dot_nt(ovl_t, term)
        imp_t = part if imp_t is None else imp_t + part
    imp = imp_t[dh:, :]
    j_s = lax.broadcasted_iota(jnp.int32, (dh, 1), 0)
    tq_l = t0 + lax.broadcasted_iota(jnp.int32, (1, qt), 1)
    cur = tq_l // SEL_BLOCK
    blk_ok = j_s * SEL_BLOCK <= tq_l
    forced = (j_s == 0) | (j_s == cur) | (j_s == cur - 1)
    score = jnp.where(blk_ok, imp + jnp.where(forced, FORCE_BONUS, 0.0), NEG_INF)
    score = jnp.where(j_s < n_sel, score, -jnp.inf)
    sc_scr[...] = score
    yield
    win_scr[...] = window_branch()
    n_grp = dh // SUBLANES
    grp = [score[k * SUBLANES:(k + 1) * SUBLANES] for k in range(n_grp)]
    rank = [jnp.zeros((SUBLANES, qt), F32) for _ in range(n_grp)]
    sub = lax.broadcasted_iota(jnp.int32, (SUBLANES, 1), 0)
    for jp in range(dh):
        row = jnp.broadcast_to(sc_scr[jp:jp + 1, :], (SUBLANES, qt))
        kj = jp // SUBLANES
        for k in range(n_grp):
            if k > kj:
                rank[k] = rank[k] + jnp.where(row >= grp[k], 1.0, 0.0)
            elif k < kj:
                rank[k] = rank[k] + jnp.where(row > grp[k], 1.0, 0.0)
            else:
                ge = jnp.where(row >= grp[k], 1.0, 0.0)
                gt = jnp.where(row > grp[k], 1.0, 0.0)
                rank[k] = rank[k] + jnp.where(sub > jp % SUBLANES, ge, gt)
    n_top = float(min(SEL_TOP, n_sel))
    sel_t = jnp.concatenate(rank, axis=0) < n_top
    sel_t = jnp.where(sel_t & blk_ok, 1.0, 0.0)
    sel = jnp.concatenate([jnp.zeros((dh, qt), F32), sel_t], axis=0).T
    yield

    m_scr[...] = jnp.full(m_scr.shape, NEG_INF, F32)
    acc_scr[...] = jnp.zeros(acc_scr.shape, F32)

    def online_step(tiles, v_slab):
        m_old = m_scr[...]
        m_new = jnp.maximum(m_old, _row_max_bcast(tiles))
        alpha = jnp.exp2(m_old - m_new)
        pexp = jnp.concatenate([jnp.exp2(t - m_new) for t in tiles], axis=1).astype(BF16)
        return m_new, alpha * acc_scr[...] + _dot(pexp, v_slab)

    def diagonal():
        d0 = pl.multiple_of(t0, qt)
        chosen = _dot(sel.astype(BF16), exp_ref[:, pl.ds(d0, qt)])
        kpos = t0 + lax.broadcasted_iota(jnp.int32, (1, qt), 1)
        okd = (chosen > 0.5) & (kpos <= tq)
        kd = ksel_ref[0, pl.ds(d0, qt), :]
        vd = vsel_ref[0, pl.ds(d0, qt), :]
        s = (_dot_nt(qp, kd).reshape(rr, qt, qt) + jnp.where(okd, 0.0, NEG_INF)[None]).reshape(m_rows, qt)
        return normalize(online_step(_lane_tiles(s), vd)[1])

    blk_lane = lane - dh
    bias = jnp.where((sel > 0.5) & (blk_lane * SEL_BLOCK < t0), 0.0, NEG_INF).astype(BF16)
    q_aug = jnp.concatenate([jnp.where(lower, qp[r * qt:(r + 1) * qt], bias) for r in range(rr)], axis=0)
    kc = SEL_KEY_CHUNK

    def step(c):
        k0 = pl.multiple_of(c * kc, kc)
        ks = ksel_ref[0, pl.ds(k0, kc), :]
        vs = vsel_ref[0, pl.ds(k0, kc), :]
        m_scr[...], acc_scr[...] = online_step(_lane_tiles(_dot_nt(q_aug, ks)), vs)

    def finish(o_sel, o_win):
        branches = (out_scr[...], o_sel, o_win)
        heads = []
        for r in range(rr):
            rows = slice(r * qt, (r + 1) * qt)
            comb = gate_scr[r * N_BRANCH] * branches[0][rows]
            for branch in range(1, N_BRANCH):
                comb = comb + gate_scr[r * N_BRANCH + branch] * branches[branch][rows]
            heads.append(comb)
        for pair in range(rr // 2):
            odd = pltpu.roll(heads[2 * pair + 1], dh, 1)
            o_ref[0, :, pair * LANES:(pair + 1) * LANES] = jnp.where(lower, heads[2 * pair], odd).astype(o_ref.dtype)

    return step, lambda: win_scr[...], diagonal, finish


def _nsa_kernel(*refs, seq):
    t0 = pl.program_id(1) * NSA_Q_TILE
    masks = _nsa_position_masks(t0, refs[1].shape[3], seq)
    pending = {g: _nsa_group(g, t0, masks, *refs, seq) for g in range(NSA_KV_HEADS)}
    done = {}
    while pending:
        for g in sorted(pending):
            try:
                next(pending[g])
            except StopIteration as stop:
                done[g] = stop.value
                del pending[g]
    groups = [done[g] for g in sorted(done)]

    def sel_step(c, carry):
        for grp in groups:
            grp[0](c)
        return carry

    lax.fori_loop(0, (t0 + SEL_KEY_CHUNK - 1) // SEL_KEY_CHUNK, sel_step, 0)
    o_win = [grp[1]() for grp in groups]
    o_sel = [grp[2]() for grp in groups]
    for grp, sel_out, win_out in zip(groups, o_sel, o_win):
        grp[3](sel_out, win_out)


def _nsa_attention(qpad, cmp_slabs, kvs, gates, batch, seq):
    qt = NSA_Q_TILE
    dh = NSA_HEAD_DIM
    n_sel = seq // SEL_BLOCK
    n_chunks = seq // CMP_STRIDE
    assert n_sel <= dh and seq % SEL_KEY_CHUNK == 0 and seq >= WINDOW + qt and SEL_KEY_CHUNK % qt == 0
    n_cmp = (seq - CMP_LEN) // CMP_STRIDE + 1
    cmp_start = np.arange(n_chunks) * CMP_STRIDE
    blk = np.arange(LANES) - dh
    overlap_t = ((cmp_start[None, :] < blk[:, None] * SEL_BLOCK + SEL_BLOCK)
                 & (cmp_start[None, :] + CMP_LEN > blk[:, None] * SEL_BLOCK)
                 & (np.arange(n_chunks)[None, :] < n_cmp) & (blk[:, None] >= 0) & (blk[:, None] < n_sel))
    overlap_t = jnp.asarray(overlap_t.astype(np.float32), BF16)
    expand = (np.arange(seq)[None, :] // SEL_BLOCK) == blk[:, None]
    expand = jnp.asarray(expand.astype(np.float32), BF16)
    m_rows = NSA_GROUP * qt
    ng = NSA_KV_HEADS
    return pl.pallas_call(
        functools.partial(_nsa_kernel, seq=seq),
        grid=(batch, seq // qt),
        in_specs=[
            pl.BlockSpec((1, qt, Q_COLS), lambda b, i: (b, i, 0)),
            pl.BlockSpec((1, ng, 2, n_chunks, LANES), lambda b, i: (b, 0, 0, 0, 0)),
            pl.BlockSpec((1, seq, KVS_COLS), lambda b, i: (b, 0, 0)),
            pl.BlockSpec((1, qt, GATE_COLS), lambda b, i: (b, i, 0)),
            pl.BlockSpec((LANES, n_chunks), lambda b, i: (0, 0)),
            pl.BlockSpec((LANES, seq), lambda b, i: (0, 0)),
        ],
        out_specs=pl.BlockSpec((1, qt, NSA_WIDTH), lambda b, i: (b, i, 0)),
        out_shape=jax.ShapeDtypeStruct((batch, seq, NSA_WIDTH), BF16),
        scratch_shapes=[
            pltpu.VMEM((ng, dh, qt), F32),
            pltpu.VMEM((ng, NSA_GROUP * N_BRANCH, qt, LANES), F32),
            pltpu.VMEM((ng, m_rows, LANES), F32),
            pltpu.VMEM((ng, m_rows, LANES), F32),
            pltpu.VMEM((ng, m_rows, LANES), F32),
            pltpu.VMEM((ng, m_rows, LANES), F32),
        ],
        compiler_params=_params(2),
        name="nsa_attention",
    )(qpad, cmp_slabs, kvs, gates, overlap_t, expand)


def _pool_out_rows(rows, x_ref, nsa_ref, u_ref, uh_ref, pw_ref, ps_ref, wo_ref, g_ref, b_ref, o_ref, seq):
    tm = x_ref.shape[0]
    n_rows = rows.stop - rows.start
    i = pl.program_id(0)
    tiles_per_seq = seq // tm
    t_seq = (i % tiles_per_seq) * tm + rows.start + lax.broadcasted_iota(jnp.int32, (n_rows, 1), 0)
    u = u_ref[rows, :]
    if rows.start == 0:
        halo = jnp.where((i % tiles_per_seq) == 0, 0.0, uh_ref[...])
    else:
        halo = u_ref[rows.start - POOL_HALO:rows.start, :]
    z = jnp.concatenate([halo, u], axis=0)
    gd = POOL_GROUP_DIM
    y = jnp.dot(nsa_ref[rows, :], wo_ref[0:NSA_WIDTH, :], preferred_element_type=F32)
    yield
    mixed = []
    for gi, w in enumerate(POOL_WINDOWS):
        run = z[:, gi * gd:(gi + 1) * gd]
        span = 1
        while span < w:
            run = run + pltpu.roll(run, span, 0)
            span *= 2
        cnt = jnp.minimum(t_seq + 1, w).astype(F32)
        pooled = run[POOL_HALO:, :] / cnt - u[:, gi * gd:(gi + 1) * gd]
        mixed.append((_dot(pooled.astype(BF16), pw_ref[gi]) * ps_ref[:, gi * gd:(gi + 1) * gd]).astype(BF16))
    yield
    y = y + _dot(jnp.concatenate(mixed, axis=1), wo_ref[NSA_WIDTH:, :])
    yield
    o_ref[rows, :] = _layer_norm(ALPHA * x_ref[rows, :] + y, g_ref[...], b_ref[...])


def _pool_out_kernel(*refs, seq):
    _run_staggered([_pool_out_rows(rows, *refs, seq) for rows in _row_slices(refs[0].shape[0])])


def _pool_out(x2d, nsa2d, u2d, pool_w, pool_scale, w_out, ln_g, ln_b, seq):
    t = x2d.shape[0]
    tm = SPLIT_ROW_TILE
    hb = tm // POOL_HALO
    return pl.pallas_call(
        functools.partial(_pool_out_kernel, seq=seq),
        grid=(t // tm,),
        in_specs=[
            pl.BlockSpec((tm, D_MODEL), lambda i: (i, 0)),
            pl.BlockSpec((tm, NSA_WIDTH), lambda i: (i, 0)),
            pl.BlockSpec((tm, POOL_WIDTH), lambda i: (i, 0)),
            pl.BlockSpec((POOL_HALO, POOL_WIDTH), lambda i: (jnp.maximum(i * hb - 1, 0), 0)),
            pl.BlockSpec(pool_w.shape, lambda i: (0, 0, 0)),
            pl.BlockSpec((1, POOL_WIDTH), lambda i: (0, 0)),
            pl.BlockSpec(w_out.shape, lambda i: (0, 0)),
            pl.BlockSpec((1, D_MODEL), lambda i: (0, 0)),
            pl.BlockSpec((1, D_MODEL), lambda i: (0, 0)),
        ],
        out_specs=pl.BlockSpec((tm, D_MODEL), lambda i: (i, 0)),
        out_shape=jax.ShapeDtypeStruct((t, D_MODEL), F32),
        compiler_params=_params(1),
        name="pool_out_ln",
    )(x2d, nsa2d, u2d, u2d, pool_w.astype(BF16), pool_scale.reshape(1, -1), w_out.astype(BF16),
      ln_g.reshape(1, -1), ln_b.reshape(1, -1))


def _sgu_rows(rows, x_ref, win_ref, ng_ref, nb_ref, ws_ref, bs_ref, wo_ref, g_ref, b_ref, o_ref, sv_scr):
    n_rows = rows.stop - rows.start
    n_chunk = n_rows // SG_CHUNK
    gd = SG_GROUP_DIM
    x = x_ref[rows, :]
    xb = x.astype(BF16)
    hu = _dot(xb, win_ref[:, 0:SG_WIDTH])
    hv = _dot(xb, win_ref[:, SG_WIDTH:2 * SG_WIDTH])
    yield
    zu = _gelu(hu)
    v = _layer_norm(_gelu(hv), ng_ref[...], nb_ref[...]).astype(BF16)
    yield
    row = lax.broadcasted_iota(jnp.int32, (SG_CHUNK, SG_CHUNK), 0)
    col = lax.broadcasted_iota(jnp.int32, (SG_CHUNK, SG_CHUNK), 1)
    causal = col <= row
    for h in range(SG_GROUPS):
        w = jnp.where(causal, ws_ref[h], 0.0).astype(BF16)
        rhs = jnp.concatenate(
            [v[c * SG_CHUNK:(c + 1) * SG_CHUNK, h * gd:(h + 1) * gd] for c in range(n_chunk)], axis=1)
        res = _dot(w, rhs)
        for c in range(n_chunk):
            r0 = rows.start + c * SG_CHUNK
            sv_scr[r0:r0 + SG_CHUNK, h * gd:(h + 1) * gd] = res[:, c * gd:(c + 1) * gd]
    bias = bs_ref[...]
    gated = jnp.concatenate(
        [zu[c * SG_CHUNK:(c + 1) * SG_CHUNK, :]
         * (sv_scr[rows.start + c * SG_CHUNK:rows.start + (c + 1) * SG_CHUNK, :] + bias) for c in range(n_chunk)],
        axis=0).astype(BF16)
    yield
    y = _dot(gated, wo_ref[...])
    yield
    o_ref[rows, :] = _layer_norm(ALPHA * x + y, g_ref[...], b_ref[...])


def _sgu_kernel(*refs):
    _run_staggered([_sgu_rows(rows, *refs) for rows in _row_slices(refs[0].shape[0])])


def _sgu(x2d, w_in, norm_g, norm_b, w_s, b_s, w_out, ln_g, ln_b):
    t = x2d.shape[0]
    tm = SPLIT_ROW_TILE
    bias = jnp.repeat(b_s.T, SG_GROUP_DIM, axis=1)
    vec = lambda a: a.reshape(1, -1)
    const2 = lambda i: (0, 0)
    return pl.pallas_call(
        _sgu_kernel,
        grid=(t // tm,),
        in_specs=[
            pl.BlockSpec((tm, D_MODEL), lambda i: (i, 0)),
            pl.BlockSpec(w_in.shape, const2),
            pl.BlockSpec((1, SG_WIDTH), const2),
            pl.BlockSpec((1, SG_WIDTH), const2),
            pl.BlockSpec(w_s.shape, lambda i: (0, 0, 0)),
            pl.BlockSpec(bias.shape, const2),
            pl.BlockSpec(w_out.shape, const2),
            pl.BlockSpec((1, D_MODEL), const2),
            pl.BlockSpec((1, D_MODEL), const2),
        ],
        out_specs=pl.BlockSpec((tm, D_MODEL), lambda i: (i, 0)),
        out_shape=jax.ShapeDtypeStruct((t, D_MODEL), F32),
        scratch_shapes=[pltpu.VMEM((tm, SG_WIDTH), F32)],
        compiler_params=_params(1),
        name="sgu_out_ln",
    )(x2d, w_in.astype(BF16), vec(norm_g), vec(norm_b), w_s, bias, w_out.astype(BF16), vec(ln_g), vec(ln_b))


def _mem_kv_kernel(m_ref, w_ref, o_ref):
    o_ref[...] = _dot(m_ref[...].astype(BF16), w_ref[...]).astype(o_ref.dtype)


def _mem_kv(mem2d, wkv):
    rows = mem2d.shape[0]
    n = wkv.shape[1]
    tm = min(ROW_TILE, rows)
    return pl.pallas_call(
        _mem_kv_kernel,
        grid=(rows // tm,),
        in_specs=[pl.BlockSpec((tm, D_MODEL), lambda i: (i, 0)), pl.BlockSpec(wkv.shape, lambda i: (0, 0))],
        out_specs=pl.BlockSpec((tm, n), lambda i: (i, 0)),
        out_shape=jax.ShapeDtypeStruct((rows, n), BF16),
        compiler_params=_params(1),
        name="mem_kv",
    )(mem2d, wkv.astype(BF16))


def _mem_attn_rows(rows, x_ref, kv_ref, wq_ref, wo_ref, g_ref, b_ref, o_ref):
    x = x_ref[rows, :]
    q = _dot(x.astype(BF16), wq_ref[...]).astype(BF16)
    yield
    hd = MEM_HEAD_DIM
    width = MEM_HEADS * hd
    outs = []
    for h in range(MEM_HEADS):
        k = kv_ref[0, :, h * hd:(h + 1) * hd]
        v = kv_ref[0, :, width + h * hd:width + (h + 1) * hd]
        s = _dot_nt(q[:, h * hd:(h + 1) * hd], k)
        e = jnp.exp(s - jnp.max(s, axis=-1, keepdims=True))
        p = e * (1.0 / jnp.sum(e, axis=-1, keepdims=True))
        outs.append(_dot(p.astype(BF16), v).astype(BF16))
    yield
    y = _dot(jnp.concatenate(outs, axis=1), wo_ref[...])
    yield
    o_ref[rows, :] = _layer_norm(ALPHA * x + y, g_ref[...], b_ref[...])


def _mem_attn_kernel(*refs):
    _run_staggered([_mem_attn_rows(rows, *refs) for rows in _row_slices(refs[0].shape[0])])


def _mem_attn(x2d, memkv, wq, wo, ln_g, ln_b, seq):
    t = x2d.shape[0]
    tm = MEM_ROW_TILE
    tiles_per_seq = seq // tm
    mem_len, kv_cols = memkv.shape[1], memkv.shape[2]
    const2 = lambda i: (0, 0)
    wq_s = (wq * MEM_HEAD_DIM ** -0.5).astype(BF16)
    return pl.pallas_call(
        _mem_attn_kernel,
        grid=(t // tm,),
        in_specs=[
            pl.BlockSpec((tm, D_MODEL), lambda i: (i, 0)),
            pl.BlockSpec((1, mem_len, kv_cols), lambda i: (i // tiles_per_seq, 0, 0)),
            pl.BlockSpec(wq.shape, const2),
            pl.BlockSpec(wo.shape, const2),
            pl.BlockSpec((1, D_MODEL), const2),
            pl.BlockSpec((1, D_MODEL), const2),
        ],
        out_specs=pl.BlockSpec((tm, D_MODEL), lambda i: (i, 0)),
        out_shape=jax.ShapeDtypeStruct((t, D_MODEL), F32),
        compiler_params=_params(1),
        name="mem_attn_ln",
    )(x2d, memkv, wq_s, wo.astype(BF16), ln_g.reshape(1, -1), ln_b.reshape(1, -1))


def _ffn_kernel(x_ref, xh_ref, wup_ref, cw_ref, cb_ref, wd_ref, g_ref, b_ref, o_ref, *, seq):
    tm = x_ref.shape[0]
    i = pl.program_id(0)
    first = (i % (seq // tm)) == 0
    x = x_ref[...]
    halo = jnp.where(first, 0.0, xh_ref[...])
    xcat = jnp.concatenate([halo, x], axis=0).astype(BF16)
    n_steps = D_FF // FFN_CHUNK

    def cols(ref, c):
        lo = c * FFN_CHUNK
        return jnp.concatenate([ref[:, lo:lo + FFN_CHUNK], ref[:, D_FF + lo:D_FF + lo + FFN_CHUNK]], axis=1)

    def up(c):
        lo = c * FFN_CHUNK
        return jnp.concatenate([_dot(xcat, wup_ref[:, lo:lo + FFN_CHUNK]),
                                _dot(xcat, wup_ref[:, D_FF + lo:D_FF + lo + FFN_CHUNK])], axis=1)

    def shift_rows(v):
        blocks = v.reshape(v.shape[0] // SUBLANES, SUBLANES, v.shape[1])
        rot = pltpu.roll(blocks, 1, 1)
        prev = jnp.concatenate([rot[-1:], rot[:-1]], axis=0)
        sub = lax.broadcasted_iota(jnp.int32, (1, SUBLANES, 1), 1)
        return jnp.where(sub == 0, prev, rot).reshape(v.shape)

    y = None
    acts = []
    h_next = up(0)
    for c in range(n_steps):
        h = h_next
        if c + 1 < n_steps:
            h_next = up(c + 1)
        cw = cols(cw_ref, c)
        out = cw[0:1, :] * h
        for j in range(1, CONV_WIDTH):
            out = cw[j:j + 1, :] * h + shift_rows(out)
        out = out[CONV_HALO:, :] + cols(cb_ref, c)
        a, gt = out[:, :FFN_CHUNK], out[:, FFN_CHUNK:]
        acts.append((a * (gt * jax.nn.sigmoid(gt))).astype(BF16))
        if len(acts) == FFN_DOWN_GROUP or c + 1 == n_steps:
            r0 = (c + 1 - len(acts)) * FFN_CHUNK
            part = _dot(jnp.concatenate(acts, axis=1), wd_ref[r0:(c + 1) * FFN_CHUNK, :])
            y = part if y is None else y + part
            acts = []
    o_ref[...] = _layer_norm(ALPHA * x + y, g_ref[...], b_ref[...])


def _ffn(x2d, w_up, conv_w, conv_b, w_down, ln_g, ln_b, seq):
    t = x2d.shape[0]
    tm = FFN_ROW_TILE
    hb = tm // CONV_HALO
    const2 = lambda i: (0, 0)
    return pl.pallas_call(
        functools.partial(_ffn_kernel, seq=seq),
        grid=(t // tm,),
        in_specs=[
            pl.BlockSpec((tm, D_MODEL), lambda i: (i, 0)),
            pl.BlockSpec((CONV_HALO, D_MODEL), lambda i: (jnp.maximum(i * hb - 1, 0), 0)),
            pl.BlockSpec(w_up.shape, const2, pipeline_mode=pl.Buffered(1)),
            pl.BlockSpec(conv_w.shape, const2),
            pl.BlockSpec((1, 2 * D_FF), const2),
            pl.BlockSpec(w_down.shape, const2, pipeline_mode=pl.Buffered(1)),
            pl.BlockSpec((1, D_MODEL), const2),
            pl.BlockSpec((1, D_MODEL), const2),
        ],
        out_specs=pl.BlockSpec((tm, D_MODEL), lambda i: (i, 0)),
        out_shape=jax.ShapeDtypeStruct((t, D_MODEL), F32),
        compiler_params=_params(1),
        name="conv_ffn_ln",
    )(x2d, x2d, w_up.astype(BF16), conv_w, conv_b.reshape(1, -1), w_down.astype(BF16),
      ln_g.reshape(1, -1), ln_b.reshape(1, -1))


def kernel(x, mem, ab_w_in, nsa_cmp_pe, nsa_cmp_w1, nsa_cmp_w2, pool_w, pool_scale, ab_w_out,
           sg_w_in, sg_norm_g, sg_norm_b, sg_w_s, sg_b_s, sg_w_out, ln_g, ln_b,
           mem_wq, mem_wkv, mem_wo, ffn_w_up, ffn_conv_w, ffn_conv_b, ffn_w_down):
    batch, seq, d = x.shape
    assert d == D_MODEL and seq % NSA_Q_TILE == 0
    assert all(seq % tile == 0 for tile in (ROW_TILE, SPLIT_ROW_TILE, MEM_ROW_TILE, FFN_ROW_TILE))
    t = batch * seq
    h = x.reshape(t, d)
    mem2d = mem.reshape(batch * mem.shape[1], d)
    for layer in range(DEPTH):
        i = layer // 2
        if layer % 2 == 0:
            qpad, kv0, kvs, gates, u = _in_proj(h, _prep_in_proj_weight(ab_w_in[i]), seq)
            kvc = _cmp_kv(kv0, nsa_cmp_pe[i], nsa_cmp_w1[i], nsa_cmp_w2[i], batch, seq)
            nsa = _nsa_attention(qpad.reshape(batch, seq, Q_COLS), kvc, kvs.reshape(batch, seq, KVS_COLS),
                                 gates.reshape(batch, seq, GATE_COLS), batch, seq)
            h = _pool_out(h, nsa.reshape(t, NSA_WIDTH), u, pool_w[i], pool_scale[i], ab_w_out[i],
                          ln_g[layer, 0], ln_b[layer, 0], seq)
        else:
            h = _sgu(h, sg_w_in[i], sg_norm_g[i], sg_norm_b[i], sg_w_s[i], sg_b_s[i], sg_w_out[i],
                     ln_g[layer, 0], ln_b[layer, 0])
        memkv = _mem_kv(mem2d, mem_wkv[layer]).reshape(batch, mem.shape[1], -1)
        h = _mem_attn(h, memkv, mem_wq[layer], mem_wo[layer], ln_g[layer, 1], ln_b[layer, 1], seq)
        h = _ffn(h, ffn_w_up[layer], ffn_conv_w[layer], ffn_conv_b[layer], ffn_w_down[layer],
                 ln_g[layer, 2], ln_b[layer, 2], seq)
    return h.reshape(batch, seq, d)
```
